```python
import math, functools
import jax
import jax.numpy as jnp
from jax import lax
import numpy as np

D_MODEL = 1024
BATCH = 8
SEQ = 2048
DEPTH = 4
DEC_BATCH = 128
DEC_SEQ = 4
PAST_LEN = 8192
PAGE_SIZE = 128

HEAD_DIM = 64
ROPE_THETA = 10000.0
NORM_EPS = 1e-6
SCALE = HEAD_DIM ** -0.5
HALF_STEP = 0.5

SWA_HEADS = 8
SWA_KV_HEADS = 2
WINDOW = 128
GDN_HEADS = 4
GDN_DK = 128
GDN_DV = 128
CONV_WIDTH = 4
GDN_CHUNK = 64
MOBA_HEADS = 16
MOBA_KV_HEADS = 4
MOBA_BLOCK = 256
MOBA_TOPK = 3
MOBA_QBLOCK = 64
D_FF = 2816
N_MOD = 9
N_EVEN = (DEPTH + 1) // 2
N_ODD = DEPTH // 2

SWA_Q = SWA_HEADS * HEAD_DIM
SWA_KV = SWA_KV_HEADS * HEAD_DIM
GDN_QK = GDN_HEADS * GDN_DK
GDN_V = GDN_HEADS * GDN_DV
GDN_CONV_CH = 2 * GDN_QK + GDN_V
EVEN_IN = SWA_Q + 2 * SWA_KV + GDN_CONV_CH + GDN_V + 2 * GDN_HEADS
EVEN_MIX = SWA_Q + GDN_V
EVEN_SPLITS = (SWA_Q, SWA_Q + SWA_KV, SWA_Q + 2 * SWA_KV, SWA_Q + 2 * SWA_KV + GDN_CONV_CH,
               SWA_Q + 2 * SWA_KV + GDN_CONV_CH + GDN_V, EVEN_IN - GDN_HEADS)
MOBA_Q = MOBA_HEADS * HEAD_DIM
MOBA_KV = MOBA_KV_HEADS * HEAD_DIM
ODD_IN = MOBA_Q + 2 * MOBA_KV

kernel_name = 'hybrid_swa_gdn_moba_macaron_adaln_step'


def rmsnorm(x, g):
    xf = x.astype(jnp.float32)
    y = xf * lax.rsqrt(jnp.mean(xf * xf, axis=-1, keepdims=True) + NORM_EPS)
    return (y * g.astype(jnp.float32)).astype(x.dtype)


def l2norm(x):
    xf = x.astype(jnp.float32)
    return xf * lax.rsqrt(jnp.sum(xf * xf, axis=-1, keepdims=True) + 1e-6)


def modulate(h, shift, scale):
    return h * (1.0 + scale) + shift


def swiglu(h, w1, w3, w2):
    return (jax.nn.silu(h @ w1) * (h @ w3)) @ w2


def rope(x, pos):
    half = x.shape[-1] // 2
    inv = ROPE_THETA ** (-jnp.arange(half, dtype=jnp.float32) / half)
    ang = pos.astype(jnp.float32)[:, None] * inv[None, :]
    cos = jnp.cos(ang)[:, None, :]
    sin = jnp.sin(ang)[:, None, :]
    xf = x.astype(jnp.float32)
    x1, x2 = xf[..., :half], xf[..., half:]
    return jnp.concatenate([x1 * cos - x2 * sin, x2 * cos + x1 * sin], axis=-1).astype(x.dtype)


def sink_softmax(s, mask, sink):
    s = jnp.where(mask, s, -jnp.inf)
    m = jnp.maximum(jnp.max(s, axis=-1, keepdims=True), sink)
    p = jnp.exp(s - m)
    return p / (jnp.sum(p, axis=-1, keepdims=True) + jnp.exp(sink - m))


def swa_prompt(q, k, v, sinks):
    B, S = q.shape[:2]
    nq = S // WINDOW
    G = SWA_HEADS // SWA_KV_HEADS
    qb = q.reshape(B, nq, WINDOW, SWA_KV_HEADS, G, HEAD_DIM)
    kb = k.reshape(B, nq, WINDOW, SWA_KV_HEADS, HEAD_DIM)
    vb = v.reshape(B, nq, WINDOW, SWA_KV_HEADS, HEAD_DIM)
    kk = jnp.concatenate([jnp.concatenate([jnp.zeros_like(kb[:, :1]), kb[:, :-1]], 1), kb], 2)
    vv = jnp.concatenate([jnp.concatenate([jnp.zeros_like(vb[:, :1]), vb[:, :-1]], 1), vb], 2)
    s = jnp.einsum('bnqhgd,bnkhd->bnhgqk', qb, kk).astype(jnp.float32) * SCALE
    qrel = WINDOW + jnp.arange(WINDOW)[:, None]
    krel = jnp.arange(2 * WINDOW)[None, :]
    band = (krel <= qrel) & (krel > qrel - WINDOW)
    has_prev = (jnp.arange(nq) > 0)[:, None, None] | (krel >= WINDOW)[None]
    mask = (band[None] & has_prev)[None, :, None, None]
    sink = sinks.astype(jnp.float32).reshape(SWA_KV_HEADS, G)[:, :, None, None]
    p = sink_softmax(s, mask, sink)
    o = jnp.einsum('bnhgqk,bnkhd->bnqhgd', p.astype(v.dtype), vv)
    return o.reshape(B, S, SWA_Q)


def swa_sample(q, k, v, kbuf, vbuf, sinks):
    Bd, T = q.shape[:2]
    G = SWA_HEADS // SWA_KV_HEADS
    kk = jnp.concatenate([kbuf.astype(k.dtype), k], 1)
    vv = jnp.concatenate([vbuf.astype(v.dtype), v], 1)
    qpos = PAST_LEN + jnp.arange(T)
    kpos = PAST_LEN - WINDOW + jnp.arange(WINDOW + T)
    mask = (kpos[None, :] <= qpos[:, None]) & (kpos[None, :] > qpos[:, None] - WINDOW)
    qg = q.reshape(Bd, T, SWA_KV_HEADS, G, HEAD_DIM)
    s = jnp.einsum('bqhgd,bkhd->bhgqk', qg, kk).astype(jnp.float32) * SCALE
    sink = sinks.astype(jnp.float32).reshape(SWA_KV_HEADS, G)[:, :, None, None]
    p = sink_softmax(s, mask, sink)
    o = jnp.einsum('bhgqk,bkhd->bqhgd', p.astype(v.dtype), vv).reshape(Bd, T, SWA_Q)
    return o, kk[:, T:], vv[:, T:]


def short_conv(x, buf, w):
    T = x.shape[1]
    xx = jnp.concatenate([buf.astype(x.dtype), x], axis=1)
    out = sum(xx[:, j:j + T] * w[j] for j in range(CONV_WIDTH))
    return jax.nn.silu(out), xx[:, T:]


def gdn_chunked(q, k, v, g, beta, s0, chunk):
    B, T, H, dk = q.shape
    dv = v.shape[-1]
    n = T // chunk

    def to_chunks(a):
        a = a.reshape((B, n, chunk, H) + a.shape[3:])
        return jnp.moveaxis(a, (1, 3), (0, 2))

    qc, kc, vc, gc, bc = (to_chunks(a.astype(jnp.float32)) for a in (q, k, v, g, beta))
    gcum = jnp.cumsum(gc, axis=-1)
    tri = jnp.tril(jnp.ones((chunk, chunk), dtype=bool))
    strict = jnp.tril(jnp.ones((chunk, chunk), dtype=bool), -1)
    decay = jnp.exp(jnp.where(tri, gcum[..., :, None] - gcum[..., None, :], -jnp.inf))
    kbeta = kc * bc[..., None]
    lmat = jnp.where(strict, jnp.einsum('nbhid,nbhjd->nbhij', kbeta, kc) * decay, 0.0)
    amat = lmat + jnp.eye(chunk, dtype=jnp.float32)
    solve = functools.partial(lax.linalg.triangular_solve, left_side=True, lower=True, unit_diagonal=True)
    w = solve(amat, kbeta * jnp.exp(gcum)[..., None])
    u = solve(amat, vc * bc[..., None])
    attn = jnp.einsum('nbhid,nbhjd->nbhij', qc, kc) * decay
    qdec = qc * jnp.exp(gcum)[..., None]
    kdec = kc * jnp.exp(gcum[..., -1:] - gcum)[..., None]
    glast = jnp.exp(gcum[..., -1])

    def step(s, inp):
        w_i, u_i, a_i, q_i, k_i, gl_i = inp
        v_new = u_i - jnp.einsum('bhcd,bhde->bhce', w_i, s)
        o_i = jnp.einsum('bhcd,bhde->bhce', q_i, s) + jnp.einsum('bhij,bhje->bhie', a_i, v_new)
        s = s * gl_i[..., None, None] + jnp.einsum('bhcd,bhce->bhde', k_i, v_new)
        return s, o_i

    s, o = lax.scan(step, s0.astype(jnp.float32), (w, u, attn, qdec, kdec, glast))
    o = jnp.moveaxis(o, (0, 2), (1, 3)).reshape(B, T, H, dv)
    return o, s


def moba_prompt(q, k, v):
    B, S = q.shape[:2]
    G = MOBA_HEADS // MOBA_KV_HEADS
    nblk = -(-S // MOBA_BLOCK)
    pad = nblk * MOBA_BLOCK - S
    kb = jnp.pad(k, ((0, 0), (0, pad), (0, 0), (0, 0))).reshape(
        B, nblk, MOBA_BLOCK, MOBA_KV_HEADS, HEAD_DIM).transpose(0, 3, 1, 2, 4)
    vb = jnp.pad(v, ((0, 0), (0, pad), (0, 0), (0, 0))).reshape(
        B, nblk, MOBA_BLOCK, MOBA_KV_HEADS, HEAD_DIM).transpose(0, 3, 1, 2, 4)
    kmean = jnp.mean(kb.astype(jnp.float32), axis=3)
    topk = min(MOBA_TOPK, nblk - 1)
    nqb = S // MOBA_QBLOCK
    qs = q.reshape(B * nqb, MOBA_QBLOCK, MOBA_KV_HEADS, G, HEAD_DIM)
    b_idx = jnp.repeat(jnp.arange(B, dtype=jnp.int32), nqb)
    q_idx = jnp.tile(jnp.arange(nqb, dtype=jnp.int32), B)
    kv_ids = jnp.arange(MOBA_KV_HEADS)[None, :, None, None]

    def one(args):
        qg, b, qi = args
        qpos = qi * MOBA_QBLOCK + jnp.arange(MOBA_QBLOCK)
        own = (qi * MOBA_QBLOCK) // MOBA_BLOCK
        k_own = kb[b, :, own]
        v_own = vb[b, :, own]
        kpos = own * MOBA_BLOCK + jnp.arange(MOBA_BLOCK)
        s_own = jnp.einsum('qhgd,hkd->qhgk', qg, k_own).astype(jnp.float32) * SCALE
        s_own = jnp.where((kpos[None, :] <= qpos[:, None])[:, None, None, :], s_own, -jnp.inf)
        if topk == 0:
            p = jax.nn.softmax(s_own, axis=-1)
            o = jnp.einsum('qhgk,hkd->qhgd', p.astype(v.dtype), v_own)
        else:
            gate = jnp.einsum('qhgd,hnd->qhgn', qg.astype(jnp.float32), kmean[b])
            gate = jnp.where(jnp.arange(nblk) < own, gate, -jnp.inf)
            sel = lax.top_k(gate, topk)[1]
            ksel = kb[b, kv_ids, sel]
            vsel = vb[b, kv_ids, sel]
            s_sel = jnp.einsum('qhgd,qhgtkd->qhgtk', qg, ksel).astype(jnp.float32) * SCALE
            s_sel = jnp.where((sel < own)[..., None], s_sel, -jnp.inf)
            s_sel = s_sel.reshape(MOBA_QBLOCK, MOBA_KV_HEADS, G, topk * MOBA_BLOCK)
            p = jax.nn.softmax(jnp.concatenate([s_sel, s_own], axis=-1), axis=-1)
            p_sel = p[..., :topk * MOBA_BLOCK].reshape(MOBA_QBLOCK, MOBA_KV_HEADS, G, topk, MOBA_BLOCK)
            p_own = p[..., topk * MOBA_BLOCK:]
            o = (jnp.einsum('qhgtk,qhgtkd->qhgd', p_sel.astype(v.dtype), vsel)
                 + jnp.einsum('qhgk,hkd->qhgd', p_own.astype(v.dtype), v_own))
        return o.reshape(MOBA_QBLOCK, MOBA_Q)

    o = lax.map(one, (qs, b_idx, q_idx))
    return o.reshape(B, S, MOBA_Q)


def moba_sample(q, k_new, v_new, pool_k, pool_v, layer, page_table):
    Bd, T = q.shape[:2]
    G = MOBA_HEADS // MOBA_KV_HEADS
    n_full = PAST_LEN // MOBA_BLOCK
    own_start = n_full * MOBA_BLOCK
    n_own_past = PAST_LEN - own_start
    ppb = MOBA_BLOCK // PAGE_SIZE
    topk = min(MOBA_TOPK, n_full)
    qpos = PAST_LEN + jnp.arange(T)
    kpos = own_start + jnp.arange(n_own_past + T)
    own_mask = (kpos[None, :] <= qpos[:, None])[:, None, None, :]
    kv_ids = jnp.arange(MOBA_KV_HEADS)[None, :, None, None, None]

    def one(args):
        qs, kn, vn, pt = args
        qg = qs.reshape(T, MOBA_KV_HEADS, G, HEAD_DIM)
        if n_own_past > 0:
            own_pages = pt[own_start // PAGE_SIZE: PAST_LEN // PAGE_SIZE]
            k_own = jnp.concatenate([pool_k[layer, own_pages].reshape(n_own_past, MOBA_KV_HEADS, HEAD_DIM).astype(kn.dtype), kn], 0)
            v_own = jnp.concatenate([pool_v[layer, own_pages].reshape(n_own_past, MOBA_KV_HEADS, HEAD_DIM).astype(vn.dtype), vn], 0)
        else:
            k_own, v_own = kn, vn
        s_own = jnp.einsum('qhgd,khd->qhgk', qg, k_own).astype(jnp.float32) * SCALE
        s_own = jnp.where(own_mask, s_own, -jnp.inf)
        if topk == 0:
            p = jax.nn.softmax(s_own, axis=-1)
            o = jnp.einsum('qhgk,khd->qhgd', p.astype(vn.dtype), v_own)
        else:
            kpast = pool_k[layer, pt[:n_full * ppb]].reshape(n_full, MOBA_BLOCK, MOBA_KV_HEADS, HEAD_DIM)
            kmean = jnp.mean(kpast.astype(jnp.float32), axis=1)
            gate = jnp.einsum('qhgd,nhd->qhgn', qg.astype(jnp.float32), kmean)
            sel = lax.top_k(gate, topk)[1]
            pages = pt[sel[..., None] * ppb + jnp.arange(ppb)]
            ksel = pool_k[layer, pages, :, kv_ids].reshape(T, MOBA_KV_HEADS, G, topk, MOBA_BLOCK, HEAD_DIM).astype(kn.dtype)
            vsel = pool_v[layer, pages, :, kv_ids].reshape(T, MOBA_KV_HEADS, G, topk, MOBA_BLOCK, HEAD_DIM).astype(vn.dtype)
            s_sel = (jnp.einsum('qhgd,qhgtkd->qhgtk', qg, ksel).astype(jnp.float32) * SCALE).reshape(
                T, MOBA_KV_HEADS, G, topk * MOBA_BLOCK)
            p = jax.nn.softmax(jnp.concatenate([s_sel, s_own], axis=-1), axis=-1)
            p_sel = p[..., :topk * MOBA_BLOCK].reshape(T, MOBA_KV_HEADS, G, topk, MOBA_BLOCK)
            p_own = p[..., topk * MOBA_BLOCK:]
            o = (jnp.einsum('qhgtk,qhgtkd->qhgd', p_sel.astype(vn.dtype), vsel)
                 + jnp.einsum('qhgk,khd->qhgd', p_own.astype(vn.dtype), v_own))
        return o.reshape(T, MOBA_Q)

    return lax.map(one, (q, k_new, v_new, page_table))


def even_mix(h, pos, prompt, w_in, w_out, sinks, conv_w, a_log, dt_bias, gnorm, kbuf, vbuf, conv_buf, s0):
    B, T, _ = h.shape
    qa, ka, va, qkv, gz, a, bt = jnp.split(h @ w_in, EVEN_SPLITS, axis=-1)
    qa = rope(qa.reshape(B, T, SWA_HEADS, HEAD_DIM), pos)
    ka = rope(ka.reshape(B, T, SWA_KV_HEADS, HEAD_DIM), pos)
    va = va.reshape(B, T, SWA_KV_HEADS, HEAD_DIM)
    if prompt:
        oa = swa_prompt(qa, ka, va, sinks)
        new_k, new_v = ka[:, -WINDOW:], va[:, -WINDOW:]
        conv_buf = jnp.zeros((B, CONV_WIDTH - 1, GDN_CONV_CH), h.dtype)
        s0 = jnp.zeros((B, GDN_HEADS, GDN_DK, GDN_DV), jnp.float32)
        chunk = GDN_CHUNK
    else:
        oa, new_k, new_v = swa_sample(qa, ka, va, kbuf, vbuf, sinks)
        chunk = T
    qkv, new_conv = short_conv(qkv, conv_buf, conv_w)
    qb, kb, vb = jnp.split(qkv, (GDN_QK, 2 * GDN_QK), axis=-1)
    qb = l2norm(qb.reshape(B, T, GDN_HEADS, GDN_DK)) * (GDN_DK ** -0.5)
    kb = l2norm(kb.reshape(B, T, GDN_HEADS, GDN_DK))
    vb = vb.reshape(B, T, GDN_HEADS, GDN_DV)
    g = -jnp.exp(a_log.astype(jnp.float32)) * jax.nn.softplus(a.astype(jnp.float32) + dt_bias.astype(jnp.float32))
    beta = jax.nn.sigmoid(bt.astype(jnp.float32))
    ob, s_new = gdn_chunked(qb, kb, vb, g, beta, s0, chunk)
    ob = rmsnorm(ob, gnorm).astype(h.dtype) * jax.nn.silu(gz.reshape(B, T, GDN_HEADS, GDN_DV))
    mixed = jnp.concatenate([oa, ob.reshape(B, T, GDN_V)], axis=-1)
    return mixed @ w_out, (new_k, new_v, new_conv, s_new.astype(h.dtype))


def odd_mix(h, pos, prompt, w_in, w_out, pool_k, pool_v, layer, page_table):
    B, T, _ = h.shape
    q, k, v = jnp.split(h @ w_in, (MOBA_Q, MOBA_Q + MOBA_KV), axis=-1)
    q = rope(q.reshape(B, T, MOBA_HEADS, HEAD_DIM), pos)
    k = rope(k.reshape(B, T, MOBA_KV_HEADS, HEAD_DIM), pos)
    v = v.reshape(B, T, MOBA_KV_HEADS, HEAD_DIM)
    if prompt:
        o = moba_prompt(q, k, v)
    else:
        o = moba_sample(q, k, v, pool_k, pool_v, layer, page_table)
    return o @ w_out, (k, v)


def run_trunk(x, c, pos, prompt, swa_k, swa_v, conv_st, gdn_st, pool_k, pool_v, page_table,
              w_ada, b_ada, norm_g, ffn_w1, ffn_w3, ffn_w2, w_in_even, w_out_even, swa_sinks,
              conv_w, a_log, dt_bias, gdn_norm_g, w_in_odd, w_out_odd, final_norm_g):
    nsk, nsv, ncv, ngd, nmk, nmv = [], [], [], [], [], []
    for l in range(DEPTH):
        mod = (jax.nn.silu(c) @ w_ada[l] + b_ada[l]).reshape(c.shape[0], N_MOD, 1, D_MODEL)
        h = modulate(rmsnorm(x, norm_g[l, 0]), mod[:, 0], mod[:, 1])
        x = x + HALF_STEP * mod[:, 2] * swiglu(h, ffn_w1[l, 0], ffn_w3[l, 0], ffn_w2[l, 0])
        h = modulate(rmsnorm(x, norm_g[l, 1]), mod[:, 3], mod[:, 4])
        if l % 2 == 0:
            e = l // 2
            o, (nk, nv, nc, ns) = even_mix(
                h, pos, prompt, w_in_even[e], w_out_even[e], swa_sinks[e], conv_w[e], a_log[e], dt_bias[e],
                gdn_norm_g[e], None if prompt else swa_k[e], None if prompt else swa_v[e],
                None if prompt else conv_st[e], None if prompt else gdn_st[e])
            nsk.append(nk)
            nsv.append(nv)
            ncv.append(nc)
            ngd.append(ns)
        else:
            j = l // 2
            o, (mk, mv) = odd_mix(h, pos, prompt, w_in_odd[j], w_out_odd[j], pool_k, pool_v, j, page_table)
            nmk.append(mk)
            nmv.append(mv)
        x = x + mod[:, 5] * o
        h = modulate(rmsnorm(x, norm_g[l, 2]), mod[:, 6], mod[:, 7])
        x = x + HALF_STEP * mod[:, 8] * swiglu(h, ffn_w1[l, 1], ffn_w3[l, 1], ffn_w2[l, 1])
    y = rmsnorm(x, final_norm_g)
    return y, jnp.stack(nsk), jnp.stack(nsv), jnp.stack(ncv), jnp.stack(ngd), jnp.stack(nmk), jnp.stack(nmv)


def setup_inputs(seed: int = 0) -> dict:
    key = jax.random.key(seed)
    ks = jax.random.split(key, 32)
    f32 = jnp.float32

    def nrm(k, shape, s=1.0):
        return jax.random.normal(k, shape, f32) * s

    n_pages = PAST_LEN // PAGE_SIZE
    n_used = DEC_BATCH * n_pages
    n_phys = n_used + n_used // 4
    page_table = jax.random.permutation(ks[8], n_phys)[:n_used].reshape(DEC_BATCH, n_pages).astype(jnp.int32)
    dt = jnp.exp(jax.random.uniform(ks[20], (N_EVEN, GDN_HEADS), f32, math.log(1e-3), math.log(1e-1)))
    return {
        'x_prompt': nrm(ks[0], (BATCH, SEQ, D_MODEL)),
        'x_sample': nrm(ks[1], (DEC_BATCH, DEC_SEQ, D_MODEL)),
        'cache_swa_k': nrm(ks[2], (N_EVEN, DEC_BATCH, WINDOW, SWA_KV_HEADS, HEAD_DIM)),
        'cache_swa_v': nrm(ks[3], (N_EVEN, DEC_BATCH, WINDOW, SWA_KV_HEADS, HEAD_DIM)),
        'state_conv': nrm(ks[4], (N_EVEN, DEC_BATCH, CONV_WIDTH - 1, GDN_CONV_CH)),
        'state_gdn': nrm(ks[5], (N_EVEN, DEC_BATCH, GDN_HEADS, GDN_DK, GDN_DV), GDN_DK ** -0.5),
        'cache_moba_k': nrm(ks[6], (N_ODD, n_phys, PAGE_SIZE, MOBA_KV_HEADS, HEAD_DIM)),
        'cache_moba_v': nrm(ks[7], (N_ODD, n_phys, PAGE_SIZE, MOBA_KV_HEADS, HEAD_DIM)),
        'page_table': page_table,
        'c_prompt': nrm(ks[9], (BATCH, D_MODEL)),
        'c_sample': nrm(ks[10], (DEC_BATCH, D_MODEL)),
        'w_ada': nrm(ks[11], (DEPTH, D_MODEL, N_MOD * D_MODEL), 0.5 * D_MODEL ** -0.5),
        'b_ada': nrm(ks[12], (DEPTH, N_MOD * D_MODEL), 0.02),
        'norm_g': 1.0 + nrm(ks[13], (DEPTH, 3, D_MODEL), 0.02),
        'ffn_w1': nrm(ks[14], (DEPTH, 2, D_MODEL, D_FF), D_MODEL ** -0.5),
        'ffn_w3': nrm(ks[15], (DEPTH, 2, D_MODEL, D_FF), D_MODEL ** -0.5),
        'ffn_w2': nrm(ks[16], (DEPTH, 2, D_FF, D_MODEL), D_FF ** -0.5),
        'w_in_even': nrm(ks[17], (N_EVEN, D_MODEL, EVEN_IN), D_MODEL ** -0.5),
        'w_out_even': nrm(ks[18], (N_EVEN, EVEN_MIX, D_MODEL), EVEN_MIX ** -0.5),
        'swa_sinks': nrm(ks[19], (N_EVEN, SWA_HEADS)),
        'conv_w': nrm(ks[21], (N_EVEN, CONV_WIDTH, GDN_CONV_CH), CONV_WIDTH ** -0.5),
        'a_log': jnp.log(jax.random.uniform(ks[22], (N_EVEN, GDN_HEADS), f32, 1.0, 16.0)),
        'dt_bias': dt + jnp.log(-jnp.expm1(-dt)),
        'gdn_norm_g': 1.0 + nrm(ks[23], (N_EVEN, GDN_DV), 0.02),
        'w_in_odd': nrm(ks[24], (N_ODD, D_MODEL, ODD_IN), D_MODEL ** -0.5),
        'w_out_odd': nrm(ks[25], (N_ODD, MOBA_Q, D_MODEL), MOBA_Q ** -0.5),
        'final_norm_g': 1.0 + nrm(ks[26], (D_MODEL,), 0.02),
    }


def reference(x_prompt, x_sample, cache_swa_k, cache_swa_v, state_conv, state_gdn, cache_moba_k, cache_moba_v,
              page_table, c_prompt, c_sample, w_ada, b_ada, norm_g, ffn_w1, ffn_w3, ffn_w2, w_in_even,
              w_out_even, swa_sinks, conv_w, a_log, dt_bias, gdn_norm_g, w_in_odd, w_out_odd, final_norm_g):
    pos_p = jnp.arange(x_prompt.shape[1], dtype=jnp.int32)
    pos_s = PAST_LEN + jnp.arange(x_sample.shape[1], dtype=jnp.int32)
    y_prompt, p_swa_k, p_swa_v, p_conv, p_gdn, p_moba_k, p_moba_v = run_trunk(
        x_prompt, c_prompt, pos_p, True, None, None, None, None, None, None, None,
        w_ada, b_ada, norm_g, ffn_w1, ffn_w3, ffn_w2, w_in_even, w_out_even, swa_sinks,
        conv_w, a_log, dt_bias, gdn_norm_g, w_in_odd, w_out_odd, final_norm_g)
    y_sample, s_swa_k, s_swa_v, s_conv, s_gdn, s_moba_k, s_moba_v = run_trunk(
        x_sample, c_sample, pos_s, False, cache_swa_k, cache_swa_v, state_conv, state_gdn,
        cache_moba_k, cache_moba_v, page_table,
        w_ada, b_ada, norm_g, ffn_w1, ffn_w3, ffn_w2, w_in_even, w_out_even, swa_sinks,
        conv_w, a_log, dt_bias, gdn_norm_g, w_in_odd, w_out_odd, final_norm_g)
    return (y_prompt, y_sample, p_swa_k, p_swa_v, p_conv, p_gdn, p_moba_k, p_moba_v,
            s_swa_k, s_swa_v, s_conv, s_gdn, s_moba_k, s_moba_v)
```

```python
import functools

import jax
import jax.numpy as jnp
from jax import lax
from jax.experimental import pallas as pl
from jax.experimental.pallas import tpu as pltpu

F32 = jnp.float32
BF16 = jnp.bfloat16

HEAD_DIM = 64
ROPE_THETA = 10000.0
NORM_EPS = 1e-6
SCALE = HEAD_DIM ** -0.5
HALF_STEP = 0.5
SWA_HEADS = 8
SWA_KV_HEADS = 2
SWA_GROUP = SWA_HEADS // SWA_KV_HEADS
WINDOW = 128
GDN_HEADS = 4
GDN_DK = 128
GDN_DV = 128
CONV_WIDTH = 4
GDN_CHUNK = 64
MOBA_HEADS = 16
MOBA_KV_HEADS = 4
MOBA_GROUP = MOBA_HEADS // MOBA_KV_HEADS
MOBA_BLOCK = 256
MOBA_TOPK = 3
PAGE_SIZE = 128
N_MOD = 9

SWA_Q = SWA_HEADS * HEAD_DIM
SWA_KV = SWA_KV_HEADS * HEAD_DIM
GDN_QK = GDN_HEADS * GDN_DK
GDN_V = GDN_HEADS * GDN_DV
GDN_CONV_CH = 2 * GDN_QK + GDN_V
EVEN_MAIN = SWA_Q + 2 * SWA_KV + GDN_CONV_CH + GDN_V
MOBA_Q = MOBA_HEADS * HEAD_DIM
MOBA_KV = MOBA_KV_HEADS * HEAD_DIM

LANES = 128
SUBLANES = 8
VMEM_LIMIT_BYTES = 56 * 1024 * 1024

NEG_INF = float("-inf")


def _params(*sem):
    return pltpu.CompilerParams(dimension_semantics=sem, vmem_limit_bytes=VMEM_LIMIT_BYTES)


def _mm(a, b):
    return jnp.dot(a.astype(BF16), b.astype(BF16), preferred_element_type=F32)


def _mm_nt(a, b):
    return lax.dot_general(a.astype(BF16), b.astype(BF16), (((1,), (1,)), ((), ())),
                           preferred_element_type=F32)


def _silu(x):
    return x * jax.nn.sigmoid(x)


def _rms(x, g):
    return x * lax.rsqrt(jnp.mean(x * x, axis=-1, keepdims=True) + NORM_EPS) * g


def _l2norm(x):
    return x * lax.rsqrt(jnp.sum(x * x, axis=-1, keepdims=True) + 1e-6)


def _softplus(x):
    return jnp.maximum(x, 0.0) + jnp.log(1.0 + jnp.exp(-jnp.abs(x)))


def _rope(y, cos, sin_signed):
    lane = lax.broadcasted_iota(jnp.int32, (y.shape[0], LANES), 1)
    first_half = (lane % HEAD_DIM) < (HEAD_DIM // 2)
    outs = []
    for c in range(y.shape[1] // LANES):
        yc = y[:, c * LANES:(c + 1) * LANES]
        swapped = jnp.where(first_half, pltpu.roll(yc, LANES - HEAD_DIM // 2, 1),
                            pltpu.roll(yc, HEAD_DIM // 2, 1))
        outs.append(yc * cos + swapped * sin_signed)
    return outs[0] if len(outs) == 1 else jnp.concatenate(outs, axis=1)


def _adaln_kernel(cp_ref, cs_ref, w_ref, b_ref, op_ref, os_ref):
    w = w_ref[...].astype(BF16)
    b = b_ref[...]
    op_ref[...] = jnp.dot(_silu(cp_ref[...]).astype(BF16), w, preferred_element_type=F32) + b
    os_ref[...] = jnp.dot(_silu(cs_ref[...]).astype(BF16), w, preferred_element_type=F32) + b


def _adaln(c_p, c_s, w_ada, b_ada):
    depth, d, n = w_ada.shape
    rp, rs = c_p.shape[0], c_s.shape[0]
    tn = d
    return pl.pallas_call(
        _adaln_kernel,
        name="adaln",
        grid=(depth, n // tn),
        in_specs=[pl.BlockSpec((rp, d), lambda l, j: (0, 0)),
                  pl.BlockSpec((rs, d), lambda l, j: (0, 0)),
                  pl.BlockSpec((None, d, tn), lambda l, j: (l, 0, j)),
                  pl.BlockSpec((None, 1, tn), lambda l, j: (l, 0, j))],
        out_specs=[pl.BlockSpec((None, rp, tn), lambda l, j: (l, 0, j)),
                   pl.BlockSpec((None, rs, tn), lambda l, j: (l, 0, j))],
        out_shape=[jax.ShapeDtypeStruct((depth, rp, n), F32),
                   jax.ShapeDtypeStruct((depth, rs, n), F32)],
        compiler_params=_params("arbitrary", "arbitrary"),
    )(c_p, c_s, w_ada, b_ada.reshape(depth, 1, n))


class _Group:
    def __init__(self, prompt, rows, seq, mod, cos, sin):
        self.prompt = prompt
        self.rows = rows
        self.seq = seq
        self.mod = mod
        self.cos = cos
        self.sin = sin

    def tile(self, want):
        limit = self.seq if self.prompt else self.rows
        t = min(want, limit)
        assert limit % t == 0 and t % SUBLANES == 0
        return t

    def mod_spec(self, layer, k, tm, d):
        if self.prompt:
            per_seq = self.seq // tm
            return pl.BlockSpec((None, None, 1, d), lambda i, *_: (layer, i // per_seq, 0, k))
        return pl.BlockSpec((None, tm, d), lambda i, *_: (layer, i, k))

    def rope_spec(self, tm):
        if self.prompt:
            per_seq = self.seq // tm
            return pl.BlockSpec((tm, LANES), lambda i, *_: (i % per_seq, 0))
        return pl.BlockSpec((tm, LANES), lambda i, *_: (i, 0))


def _ffn_kernel(x_ref, sh_ref, sc_ref, gt_ref, g_ref, w1_ref, w3_ref, w2_ref, o_ref, h_scr, acc_scr):
    j = pl.program_id(1)

    @pl.when(j == 0)
    def _():
        h = _rms(x_ref[...], g_ref[...]) * (1.0 + sc_ref[...]) + sh_ref[...]
        h_scr[...] = h.astype(BF16)
        acc_scr[...] = jnp.zeros_like(acc_scr)

    h = h_scr[...]
    a = jnp.dot(h, w1_ref[...], preferred_element_type=F32)
    b = jnp.dot(h, w3_ref[...], preferred_element_type=F32)
    acc_scr[...] += jnp.dot((_silu(a) * b).astype(BF16), w2_ref[...], preferred_element_type=F32)

    @pl.when(j == pl.num_programs(1) - 1)
    def _():
        o_ref[...] = x_ref[...] + HALF_STEP * gt_ref[...] * acc_scr[...]


def _ffn(grp, x, layer, which, kbase, norm_g, w1, w3, w2):
    m, d = x.shape
    f = w1.shape[-1]
    tm = grp.tile(1024)
    tf = 256 if f % 256 == 0 else LANES
    wspec_in = pl.BlockSpec((None, None, d, tf), lambda i, j: (layer, which, 0, j))
    return pl.pallas_call(
        _ffn_kernel,
        name="ffn",
        grid=(m // tm, f // tf),
        in_specs=[pl.BlockSpec((tm, d), lambda i, j: (i, 0)),
                  grp.mod_spec(layer, kbase, tm, d),
                  grp.mod_spec(layer, kbase + 1, tm, d),
                  grp.mod_spec(layer, kbase + 2, tm, d),
                  pl.BlockSpec((None, None, 1, d), lambda i, j: (layer, 2 * which, 0, 0)),
                  wspec_in, wspec_in,
                  pl.BlockSpec((None, None, tf, d), lambda i, j: (layer, which, j, 0))],
        out_specs=pl.BlockSpec((tm, d), lambda i, j: (i, 0)),
        out_shape=jax.ShapeDtypeStruct((m, d), F32),
        scratch_shapes=[pltpu.VMEM((tm, d), BF16), pltpu.VMEM((tm, d), F32)],
        compiler_params=_params("parallel", "arbitrary"),
    )(x, grp.mod, grp.mod, grp.mod, norm_g, w1, w3, w2)


def _proj_kernel(segs, x_ref, sh_ref, sc_ref, g_ref, w_ref, cos_ref, sin_ref, *out_refs):
    h = (_rms(x_ref[...], g_ref[...]) * (1.0 + sc_ref[...]) + sh_ref[...]).astype(BF16)
    for (start, width, rope), o_ref in zip(segs, out_refs):
        y = jnp.dot(h, w_ref[:, start:start + width], preferred_element_type=F32)
        if rope:
            y = _rope(y, cos_ref[...], sin_ref[...])
        o_ref[...] = y


def _proj(grp, x, layer, kbase, norm_g, w, widx, segs):
    m, d = x.shape
    n = w.shape[-1]
    tm = grp.tile(512)
    return pl.pallas_call(
        functools.partial(_proj_kernel, segs),
        name="mixer_in_proj",
        grid=(m // tm,),
        in_specs=[pl.BlockSpec((tm, d), lambda i: (i, 0)),
                  grp.mod_spec(layer, kbase, tm, d),
                  grp.mod_spec(layer, kbase + 1, tm, d),
                  pl.BlockSpec((None, None, 1, d), lambda i: (layer, 1, 0, 0)),
                  pl.BlockSpec((None, d, n), lambda i: (widx, 0, 0)),
                  grp.rope_spec(tm), grp.rope_spec(tm)],
        out_specs=[pl.BlockSpec((tm, width), lambda i: (i, 0)) for _, width, _ in segs],
        out_shape=[jax.ShapeDtypeStruct((m, width), F32) for _, width, _ in segs],
        compiler_params=_params("parallel"),
    )(x, grp.mod, grp.mod, norm_g, w, grp.cos, grp.sin)


def _outproj_kernel(n_in, x_ref, gt_ref, *refs):
    a_refs, w_refs, o_ref = refs[:n_in], refs[n_in:2 * n_in], refs[2 * n_in]
    acc = _mm(a_refs[0][...], w_refs[0][...])
    for a_ref, w_ref in zip(a_refs[1:], w_refs[1:]):
        acc = acc + _mm(a_ref[...], w_ref[...])
    o_ref[...] = x_ref[...] + gt_ref[...] * acc


def _outproj(grp, x, layer, kgate, acts, w, widx):
    m, d = x.shape
    tm = grp.tile(512)
    widths = [a.shape[1] for a in acts]
    assert sum(widths) == w.shape[1]
    blocks = [sum(widths[:i]) // widths[i] for i in range(len(acts))]
    return pl.pallas_call(
        functools.partial(_outproj_kernel, len(acts)),
        name="mixer_out_proj",
        grid=(m // tm,),
        in_specs=([pl.BlockSpec((tm, d), lambda i: (i, 0)), grp.mod_spec(layer, kgate, tm, d)]
                  + [pl.BlockSpec((tm, wd), lambda i: (i, 0)) for wd in widths]
                  + [pl.BlockSpec((None, wd, d), functools.partial(lambda blk, i: (widx, blk, 0), blk))
                     for wd, blk in zip(widths, blocks)]),
        out_specs=pl.BlockSpec((tm, d), lambda i: (i, 0)),
        out_shape=jax.ShapeDtypeStruct((m, d), F32),
        compiler_params=_params("parallel"),
    )(x, grp.mod, *acts, *([w] * len(acts)))


def _final_norm_kernel(x_ref, g_ref, o_ref):
    o_ref[...] = _rms(x_ref[...], g_ref[...])


def _final_norm(grp, x, g):
    m, d = x.shape
    tm = grp.tile(1024)
    return pl.pallas_call(
        _final_norm_kernel,
        name="final_norm",
        grid=(m // tm,),
        in_specs=[pl.BlockSpec((tm, d), lambda i: (i, 0)), pl.BlockSpec((1, d), lambda i: (0, 0))],
        out_specs=pl.BlockSpec((tm, d), lambda i: (i, 0)),
        out_shape=jax.ShapeDtypeStruct((m, d), F32),
        compiler_params=_params("parallel"),
    )(x, g)


def _sink_attention(s, valid, sink, v16):
    s = jnp.where(valid, s, NEG_INF)
    m = jnp.maximum(jnp.max(s, axis=-1, keepdims=True), sink)
    p = jnp.exp(s - m)
    den = jnp.sum(p, axis=-1, keepdims=True) + jnp.exp(sink - m)
    return jnp.dot(p.astype(BF16), v16, preferred_element_type=F32) / den


def _swa_prompt_kernel(sink_ref, qk_ref, kprev_ref, v_ref, vprev_ref, o_ref):
    n = pl.program_id(1)
    w = WINDOW
    qk = qk_ref[...]
    kcat = jnp.concatenate([kprev_ref[...], qk[:, SWA_Q:SWA_Q + SWA_KV]], axis=0)
    vcat = jnp.concatenate([vprev_ref[...], v_ref[...]], axis=0)
    qi = lax.broadcasted_iota(jnp.int32, (w, 2 * w), 0)
    kj = lax.broadcasted_iota(jnp.int32, (w, 2 * w), 1)
    valid = (kj <= qi + w) & (kj > qi) & ((kj >= w) | (n > 0))
    for h in range(SWA_KV_HEADS):
        k16 = kcat[:, h * HEAD_DIM:(h + 1) * HEAD_DIM].astype(BF16)
        v16 = vcat[:, h * HEAD_DIM:(h + 1) * HEAD_DIM].astype(BF16)
        for g in range(SWA_GROUP):
            hh = h * SWA_GROUP + g
            q = qk[:, hh * HEAD_DIM:(hh + 1) * HEAD_DIM]
            s = _mm_nt(q, k16) * SCALE
            o_ref[:, hh * HEAD_DIM:(hh + 1) * HEAD_DIM] = _sink_attention(s, valid, sink_ref[hh], v16)


def _swa_prompt(qk, v, sinks, batch, seq):
    nq = seq // WINDOW
    kblk = SWA_Q // SWA_KV

    def cur(b, n):
        return (b * nq + n, 0)

    def prev(b, n):
        return (b * nq + jnp.maximum(n - 1, 0), 0)

    return pl.pallas_call(
        _swa_prompt_kernel,
        name="swa_prompt",
        grid=(batch, nq),
        in_specs=[pl.BlockSpec(memory_space=pltpu.SMEM),
                  pl.BlockSpec((WINDOW, SWA_Q + SWA_KV), cur),
                  pl.BlockSpec((WINDOW, SWA_KV), lambda b, n: (b * nq + jnp.maximum(n - 1, 0), kblk)),
                  pl.BlockSpec((WINDOW, SWA_KV), cur),
                  pl.BlockSpec((WINDOW, SWA_KV), prev)],
        out_specs=pl.BlockSpec((WINDOW, SWA_Q), cur),
        out_shape=jax.ShapeDtypeStruct((batch * seq, SWA_Q), F32),
        compiler_params=_params("parallel", "parallel"),
    )(sinks, qk, qk, v, v)


def _swa_sample_kernel(t_new, sink_ref, qk_ref, v_ref, ck_ref, cv_ref, o_ref, nk_ref, nv_ref):
    per_tile = SUBLANES // t_new
    n_tiles = qk_ref.shape[0] // SUBLANES
    w = WINDOW
    nkeys = w + SUBLANES
    r8 = lax.broadcasted_iota(jnp.int32, (SUBLANES, SWA_KV), 0)
    rows = SWA_GROUP * SUBLANES
    rr = lax.broadcasted_iota(jnp.int32, (rows, nkeys), 0)
    jj = lax.broadcasted_iota(jnp.int32, (rows, nkeys), 1)
    r = rr % SUBLANES
    t = r % t_new
    rcol = lax.broadcasted_iota(jnp.int32, (rows, 1), 0)
    r8o = lax.broadcasted_iota(jnp.int32, (SUBLANES, HEAD_DIM), 0)
    for p in range(n_tiles):
        qk8 = qk_ref[p * SUBLANES:(p + 1) * SUBLANES, :]
        v8 = v_ref[p * SUBLANES:(p + 1) * SUBLANES, :]
        knew8 = qk8[:, SWA_Q:SWA_Q + SWA_KV]
        o_seq = []
        for s in range(per_tile):
            i = p * per_tile + s
            kb = ck_ref[i]
            vb = cv_ref[i]
            shift = (SUBLANES - t_new - s * t_new) % SUBLANES
            src_k = knew8 if shift == 0 else pltpu.roll(knew8, shift, 0)
            src_v = v8 if shift == 0 else pltpu.roll(v8, shift, 0)
            rk = pltpu.roll(kb, w - t_new, 0)
            rv = pltpu.roll(vb, w - t_new, 0)
            nk_ref[i, 0:w - SUBLANES, :] = rk[0:w - SUBLANES]
            nv_ref[i, 0:w - SUBLANES, :] = rv[0:w - SUBLANES]
            nk_ref[i, w - SUBLANES:w, :] = jnp.where(r8 >= SUBLANES - t_new, src_k, rk[w - SUBLANES:w])
            nv_ref[i, w - SUBLANES:w, :] = jnp.where(r8 >= SUBLANES - t_new, src_v, rv[w - SUBLANES:w])
            kext = jnp.concatenate([kb, knew8], axis=0)
            vext = jnp.concatenate([vb, v8], axis=0)
            jn = jj - w
            valid = ((jj < w) & (jj > t)) | ((jj >= w) & ((jn // t_new) == s) & ((jn % t_new) <= t))
            outs = []
            for h in range(SWA_KV_HEADS):
                k16 = kext[:, h * HEAD_DIM:(h + 1) * HEAD_DIM].astype(BF16)
                v16 = vext[:, h * HEAD_DIM:(h + 1) * HEAD_DIM].astype(BF16)
                qh = jnp.concatenate(
                    [qk8[:, (h * SWA_GROUP + g) * HEAD_DIM:(h * SWA_GROUP + g + 1) * HEAD_DIM]
                     for g in range(SWA_GROUP)], axis=0)
                sinkv = jnp.zeros((rows, 1), F32)
                for g in range(SWA_GROUP):
                    sinkv = jnp.where(rcol // SUBLANES == g, sink_ref[h * SWA_GROUP + g], sinkv)
                s_ = _mm_nt(qh, k16) * SCALE
                outs.append(_sink_attention(s_, valid, sinkv, v16))
            o_seq.append(outs)
        for h in range(SWA_KV_HEADS):
            for g in range(SWA_GROUP):
                tile = o_seq[0][h][g * SUBLANES:(g + 1) * SUBLANES]
                for s in range(1, per_tile):
                    tile = jnp.where(r8o // t_new == s, o_seq[s][h][g * SUBLANES:(g + 1) * SUBLANES], tile)
                hh = h * SWA_GROUP + g
                o_ref[p * SUBLANES:(p + 1) * SUBLANES, hh * HEAD_DIM:(hh + 1) * HEAD_DIM] = tile


def _swa_sample(qk, v, ck, cv, sinks, layer_e, nseq, t_new):
    assert SUBLANES % t_new == 0
    bb = SUBLANES
    while nseq % bb:
        bb //= 2
    assert (bb * t_new) % SUBLANES == 0
    rows = bb * t_new
    cache_spec = pl.BlockSpec((None, bb, WINDOW, SWA_KV), lambda i: (layer_e, i, 0, 0))
    out_cache = pl.BlockSpec((bb, WINDOW, SWA_KV), lambda i: (i, 0, 0))
    return pl.pallas_call(
        functools.partial(_swa_sample_kernel, t_new),
        name="swa_sample",
        grid=(nseq // bb,),
        in_specs=[pl.BlockSpec(memory_space=pltpu.SMEM),
                  pl.BlockSpec((rows, SWA_Q + SWA_KV), lambda i: (i, 0)),
                  pl.BlockSpec((rows, SWA_KV), lambda i: (i, 0)),
                  cache_spec, cache_spec],
        out_specs=[pl.BlockSpec((rows, SWA_Q), lambda i: (i, 0)), out_cache, out_cache],
        out_shape=[jax.ShapeDtypeStruct((nseq * t_new, SWA_Q), F32),
                   jax.ShapeDtypeStruct((nseq, WINDOW, SWA_KV), F32),
                   jax.ShapeDtypeStruct((nseq, WINDOW, SWA_KV), F32)],
        compiler_params=_params("parallel"),
    )(sinks, qk, v, ck, cv)


def _gdn_gates(ab, alog_row, dtb_row):
    g_all = -jnp.exp(alog_row) * _softplus(ab + dtb_row)
    beta_all = jax.nn.sigmoid(ab)
    return g_all, beta_all


def _cumsum_rows(x):
    n = x.shape[0]
    r = lax.broadcasted_iota(jnp.int32, x.shape, 0)
    s = 1
    while s < n:
        x = x + jnp.where(r >= s, pltpu.roll(x, s, 0), 0.0)
        s *= 2
    return x


GDN_INV_BASE = 8


def _unit_lower_inverse(lmat, ri, cj, eye):
    n = lmat.shape[0]
    bsz = GDN_INV_BASE
    same = (ri // bsz) == (cj // bsz)
    pw = jnp.where(same, -lmat, 0.0)
    inv = eye + pw
    for _ in range((bsz - 1).bit_length() - 1):
        pw = _mm(pw, pw)
        inv = inv + _mm(inv, pw)
    while bsz < n:
        merged = (ri // (2 * bsz)) == (cj // (2 * bsz))
        off = jnp.where(merged & jnp.logical_not(same), lmat, 0.0)
        inv = inv - _mm(inv, _mm(off, inv))
        same = merged
        bsz *= 2
    return inv


def _gdn_prompt_kernel(x_ref, gz_ref, ab_ref, cw_ref, alog_ref, dtb_ref, gn_ref, o_ref, s_out_ref,
                       xs_scr, s_scr):
    c = pl.program_id(1)
    ch = GDN_CHUNK

    @pl.when(c == 0)
    def _():
        xs_scr[0:SUBLANES, :] = jnp.zeros((SUBLANES, GDN_CONV_CH), F32)
        s_scr[...] = jnp.zeros_like(s_scr)

    xs_scr[SUBLANES:SUBLANES + ch, :] = x_ref[...]
    cw = cw_ref[...]
    base = SUBLANES - (CONV_WIDTH - 1)
    conv = xs_scr[base:base + ch, :] * cw[0:1, :]
    for j in range(1, CONV_WIDTH):
        conv = conv + xs_scr[base + j:base + j + ch, :] * cw[j:j + 1, :]
    xs_scr[0:SUBLANES, :] = xs_scr[ch:ch + SUBLANES, :]
    qkv = _silu(conv)

    g_all, beta_all = _gdn_gates(ab_ref[...], alog_ref[...], dtb_ref[...])
    gcum = _cumsum_rows(g_all)
    gcum_t = gcum.T
    ri = lax.broadcasted_iota(jnp.int32, (ch, ch), 0)
    cj = lax.broadcasted_iota(jnp.int32, (ch, ch), 1)
    eye = (ri == cj).astype(F32)

    for h in range(GDN_HEADS):
        q = _l2norm(qkv[:, h * GDN_DK:(h + 1) * GDN_DK]) * (GDN_DK ** -0.5)
        k = _l2norm(qkv[:, GDN_QK + h * GDN_DK:GDN_QK + (h + 1) * GDN_DK])
        v = qkv[:, 2 * GDN_QK + h * GDN_DV:2 * GDN_QK + (h + 1) * GDN_DV]
        gc = gcum[:, h:h + 1]
        gr = gcum_t[h:h + 1, :]
        beta = beta_all[:, GDN_HEADS + h:GDN_HEADS + h + 1]
        decay = jnp.where(cj <= ri, jnp.exp(gc - gr), 0.0)
        eg = jnp.exp(gc)
        kbeta = k * beta
        lmat = jnp.where(cj < ri, _mm_nt(kbeta, k) * decay, 0.0)
        inv = _unit_lower_inverse(lmat, ri, cj, eye)
        wu = _mm(inv, jnp.concatenate([kbeta * eg, v * beta], axis=1))
        w_ = wu[:, :GDN_DK]
        u_ = wu[:, GDN_DK:]
        attn = _mm_nt(q, k) * decay
        s_prev = s_scr[h]
        v_new = u_ - _mm(w_, s_prev)
        o = _mm(q * eg, s_prev) + _mm(attn, v_new)
        glast = gcum[ch - 1:ch, h:h + 1]
        kdec = k * jnp.exp(glast - gc)
        s_scr[h] = s_prev * jnp.exp(glast) + _mm(kdec.T, v_new)
        gz = gz_ref[:, h * GDN_DV:(h + 1) * GDN_DV]
        o_ref[:, h * GDN_DV:(h + 1) * GDN_DV] = _rms(o, gn_ref[...]) * _silu(gz)

    @pl.when(c == pl.num_programs(1) - 1)
    def _():
        s_out_ref[...] = s_scr[...]


def _gdn_prompt(conv_in, gz, ab, cw, alog, dtb, gn, layer_e, batch, seq):
    ch = GDN_CHUNK
    nch = seq // ch

    def rows(b, c):
        return (b * nch + c, 0)

    def par(shape):
        return pl.BlockSpec((None,) + shape, lambda b, c: (layer_e, 0, 0))

    return pl.pallas_call(
        _gdn_prompt_kernel,
        name="gdn_prompt",
        grid=(batch, nch),
        in_specs=[pl.BlockSpec((ch, GDN_CONV_CH), rows),
                  pl.BlockSpec((ch, GDN_V), rows),
                  pl.BlockSpec((ch, LANES), rows),
                  par((CONV_WIDTH, GDN_CONV_CH)), par((1, LANES)), par((1, LANES)), par((1, GDN_DV))],
        out_specs=[pl.BlockSpec((ch, GDN_V), rows),
                   pl.BlockSpec((None, GDN_HEADS, GDN_DK, GDN_DV), lambda b, c: (b, 0, 0, 0))],
        out_shape=[jax.ShapeDtypeStruct((batch * seq, GDN_V), F32),
                   jax.ShapeDtypeStruct((batch, GDN_HEADS, GDN_DK, GDN_DV), F32)],
        scratch_shapes=[pltpu.VMEM((SUBLANES + ch, GDN_CONV_CH), F32),
                        pltpu.VMEM((GDN_HEADS, GDN_DK, GDN_DV), F32)],
        compiler_params=_params("parallel", "arbitrary"),
    )(conv_in, gz, ab, cw, alog, dtb, gn)


def _gdn_sample_kernel(t_new, xx_ref, gz_ref, ab_ref, s_ref, cw_ref, alog_ref, dtb_ref, gn_ref,
                       o_ref, so_ref):
    first = SUBLANES - t_new
    cw = cw_ref[...]
    gn = gn_ref[...]

    def one(i, carry):
        xx = xx_ref[i]
        conv = xx * cw[CONV_WIDTH - 1:CONV_WIDTH, :]
        for d in range(1, CONV_WIDTH):
            conv = conv + pltpu.roll(xx, d, 0) * cw[CONV_WIDTH - 1 - d:CONV_WIDTH - d, :]
        qkv = _silu(conv)
        g_all, beta_all = _gdn_gates(ab_ref[i], alog_ref[...], dtb_ref[...])
        a_all = jnp.exp(g_all)
        gz = gz_ref[i]
        o_ref[i, 0:first, :] = jnp.zeros((first, GDN_V), F32)
        for h in range(GDN_HEADS):
            q = _l2norm(qkv[:, h * GDN_DK:(h + 1) * GDN_DK]) * (GDN_DK ** -0.5)
            k = _l2norm(qkv[:, GDN_QK + h * GDN_DK:GDN_QK + (h + 1) * GDN_DK])
            v = qkv[:, 2 * GDN_QK + h * GDN_DV:2 * GDN_QK + (h + 1) * GDN_DV]
            q_t = q.T
            k_t = k.T
            s = s_ref[i, h]
            for tok in range(t_new):
                r = first + tok
                kc = k_t[:, r:r + 1]
                qc = q_t[:, r:r + 1]
                a = a_all[r:r + 1, h:h + 1]
                b = beta_all[r:r + 1, GDN_HEADS + h:GDN_HEADS + h + 1]
                kts = jnp.sum(s * kc, axis=0, keepdims=True)
                s = a * s + kc * (b * (v[r:r + 1, :] - a * kts))
                o = jnp.sum(s * qc, axis=0, keepdims=True)
                o_ref[i, r:r + 1, h * GDN_DV:(h + 1) * GDN_DV] = (
                    _rms(o, gn) * _silu(gz[r:r + 1, h * GDN_DV:(h + 1) * GDN_DV]))
            so_ref[i, h] = s
        return carry

    lax.fori_loop(0, xx_ref.shape[0], one, 0)


def _gdn_sample(xx8, gz8, ab8, state, cw, alog, dtb, gn, layer_e, nseq, t_new):
    bb = 16
    while nseq % bb:
        bb //= 2

    def par(shape):
        return pl.BlockSpec((None,) + shape, lambda i: (layer_e, 0, 0))

    return pl.pallas_call(
        functools.partial(_gdn_sample_kernel, t_new),
        name="gdn_sample",
        grid=(nseq // bb,),
        in_specs=[pl.BlockSpec((bb, SUBLANES, GDN_CONV_CH), lambda i: (i, 0, 0)),
                  pl.BlockSpec((bb, SUBLANES, GDN_V), lambda i: (i, 0, 0)),
                  pl.BlockSpec((bb, SUBLANES, LANES), lambda i: (i, 0, 0)),
                  pl.BlockSpec((None, bb, GDN_HEADS, GDN_DK, GDN_DV), lambda i: (layer_e, i, 0, 0, 0)),
                  par((CONV_WIDTH, GDN_CONV_CH)), par((1, LANES)), par((1, LANES)), par((1, GDN_DV))],
        out_specs=[pl.BlockSpec((bb, SUBLANES, GDN_V), lambda i: (i, 0, 0)),
                   pl.BlockSpec((bb, GDN_HEADS, GDN_DK, GDN_DV), lambda i: (i, 0, 0, 0))],
        out_shape=[jax.ShapeDtypeStruct((nseq, SUBLANES, GDN_V), F32),
                   jax.ShapeDtypeStruct((nseq, GDN_HEADS, GDN_DK, GDN_DV), F32)],
        compiler_params=_params("parallel"),
    )(xx8, gz8, ab8, state, cw, alog, dtb, gn)


def _moba_prompt_kernel(nblk, topk, q_ref, k_ref, vt_ref, o_ref, kmean_scr, m_scr, l_scr, acc_scr):
    qt = pl.program_id(1)
    blk = MOBA_BLOCK
    tq = q_ref.shape[0]
    nq = MOBA_GROUP * tq
    nrow = kmean_scr.shape[0]

    @pl.when(qt == 0)
    def _():
        kmean_scr[...] = jnp.zeros_like(kmean_scr)
        for n in range(nblk):
            kmean_scr[n:n + 1, :] = jnp.mean(k_ref[n * blk:(n + 1) * blk, :], axis=0, keepdims=True)

    own = qt
    own_start = pl.multiple_of(own * blk, blk)
    nidx = lax.broadcasted_iota(jnp.int32, (nrow, nq), 0)
    kk = lax.broadcasted_iota(jnp.int32, (blk, nq), 0)
    qi = lax.broadcasted_iota(jnp.int32, (blk, nq), 1) % tq

    for h in range(MOBA_KV_HEADS):
        lanes = slice(h * HEAD_DIM, (h + 1) * HEAD_DIM)
        qh = jnp.concatenate(
            [q_ref[:, (h * MOBA_GROUP + g) * HEAD_DIM:(h * MOBA_GROUP + g + 1) * HEAD_DIM]
             for g in range(MOBA_GROUP)], axis=0)
        qh16 = qh.astype(BF16)

        s_t = _mm_nt(k_ref[pl.ds(own_start, blk), lanes], qh16) * SCALE
        s_t = jnp.where(kk <= qi, s_t, NEG_INF)
        m0 = jnp.max(s_t, axis=0, keepdims=True)
        p = jnp.exp(s_t - m0)
        m_scr[...] = m0
        l_scr[...] = jnp.sum(p, axis=0, keepdims=True)
        acc_scr[...] = _mm(vt_ref[lanes, pl.ds(own_start, blk)], p)

        if topk > 0:
            gate_t = _mm_nt(kmean_scr[:, lanes], qh)
            gm = jnp.where(nidx < own, gate_t, NEG_INF)
            sel_t = jnp.zeros((nrow, nq), F32)
            for n in range(nblk - 1):
                row = gm[n:n + 1, :]
                beats = jnp.where(gm > row, 1.0, jnp.where((gm == row) & (nidx < n), 1.0, 0.0))
                cnt = jnp.sum(beats, axis=0, keepdims=True)
                keep = jnp.where(cnt < topk, 1.0, 0.0) * jnp.where(n < own, 1.0, 0.0)
                sel_t = jnp.where(nidx == n, keep, sel_t)

            def past(n, carry):
                start = pl.multiple_of(n * blk, blk)
                sel_row = jnp.sum(jnp.where(nidx == n, sel_t, 0.0), axis=0, keepdims=True)
                s_n = _mm_nt(k_ref[pl.ds(start, blk), lanes], qh16) * SCALE
                s_n = jnp.where(sel_row > 0.0, s_n, NEG_INF)
                m_old = m_scr[...]
                m_new = jnp.maximum(m_old, jnp.max(s_n, axis=0, keepdims=True))
                alpha = jnp.exp(m_old - m_new)
                p_n = jnp.exp(s_n - m_new)
                l_scr[...] = alpha * l_scr[...] + jnp.sum(p_n, axis=0, keepdims=True)
                acc_scr[...] = alpha * acc_scr[...] + _mm(vt_ref[lanes, pl.ds(start, blk)], p_n)
                m_scr[...] = m_new
                return carry

            lax.fori_loop(0, own, past, 0)

        o_t = acc_scr[...] / l_scr[...]
        for pair in range(MOBA_GROUP // 2):
            two = jnp.concatenate([o_t[:, (2 * pair) * tq:(2 * pair + 1) * tq],
                                   o_t[:, (2 * pair + 1) * tq:(2 * pair + 2) * tq]], axis=0)
            col = h * MOBA_GROUP * HEAD_DIM + pair * LANES
            o_ref[:, col:col + LANES] = two.T


def _moba_prompt(q, k, vt, batch, seq):
    blk = MOBA_BLOCK
    assert seq % blk == 0
    nblk = seq // blk
    topk = min(MOBA_TOPK, nblk - 1)
    nrow = -(-nblk // SUBLANES) * SUBLANES
    nq = MOBA_GROUP * blk
    return pl.pallas_call(
        functools.partial(_moba_prompt_kernel, nblk, topk),
        name="moba_prompt",
        grid=(batch, nblk),
        in_specs=[pl.BlockSpec((blk, MOBA_Q), lambda b, t: (b * nblk + t, 0)),
                  pl.BlockSpec((seq, MOBA_KV), lambda b, t: (b, 0)),
                  pl.BlockSpec((None, MOBA_KV, seq), lambda b, t: (b, 0, 0))],
        out_specs=pl.BlockSpec((blk, MOBA_Q), lambda b, t: (b * nblk + t, 0)),
        out_shape=jax.ShapeDtypeStruct((batch * seq, MOBA_Q), F32),
        scratch_shapes=[pltpu.VMEM((nrow, MOBA_KV), F32), pltpu.VMEM((1, nq), F32),
                        pltpu.VMEM((1, nq), F32), pltpu.VMEM((HEAD_DIM, nq), F32)],
        compiler_params=_params("parallel", "arbitrary"),
    )(q, k, vt)


def _moba_sample_kernel(layer, n_pages, topk, t_new, pt_ref, q_ref, kn_ref, vn_ref, pk_ref, pv_ref,
                        o_ref, kbuf, vbuf, s_scr, sem):
    b = pl.program_id(0)
    nb = pl.num_programs(0)
    blk = MOBA_BLOCK
    ppb = blk // PAGE_SIZE
    n_full = n_pages // ppb
    nq = q_ref.shape[1]

    def page_copies(seq_idx, slot, p):
        pg = pt_ref[seq_idx * n_pages + p]
        return (pltpu.make_async_copy(pk_ref.at[layer, pg], kbuf.at[slot, p], sem.at[0, slot]),
                pltpu.make_async_copy(pv_ref.at[layer, pg], vbuf.at[slot, p], sem.at[1, slot]))

    def fetch(seq_idx, slot):
        for p in range(n_pages):
            ck, cv = page_copies(seq_idx, slot, p)
            ck.start()
            cv.start()

    slot = b % 2

    @pl.when(b == 0)
    def _():
        fetch(0, 0)

    @pl.when(b + 1 < nb)
    def _():
        fetch(b + 1, 1 - slot)

    for p in range(n_pages):
        ck, cv = page_copies(b, slot, p)
        ck.wait()
        cv.wait()

    q16 = q_ref[0].astype(BF16)
    lane = lax.broadcasted_iota(jnp.int32, (nq, LANES), 1)

    s_own = _mm_nt(q16, kn_ref[0]) * SCALE
    rr = lax.broadcasted_iota(jnp.int32, s_own.shape, 0)
    jj = lax.broadcasted_iota(jnp.int32, s_own.shape, 1)
    tq = (rr % (t_new * MOBA_GROUP)) // MOBA_GROUP
    s_own = jnp.where((jj <= tq) & (jj < t_new), s_own, NEG_INF)
    m = jnp.max(s_own, axis=1, keepdims=True)

    if topk > 0:
        def pass1(n, carry):
            gate, bmax = carry
            kn = kbuf[slot, pl.ds(n * ppb, ppb)].reshape(blk, MOBA_KV)
            s = _mm_nt(q16, kn)
            s_scr[:, pl.ds(pl.multiple_of(n * blk, blk), blk)] = s
            gsum = jnp.sum(s, axis=1, keepdims=True) * (1.0 / blk)
            gate = jnp.where(lane == n, gsum, gate)
            bmax = jnp.where(lane == n, jnp.max(s, axis=1, keepdims=True), bmax)
            return gate, bmax

        neg = jnp.full((nq, LANES), NEG_INF, F32)
        gate, bmax = lax.fori_loop(0, n_full, pass1, (neg, neg))
        sel = jnp.zeros((nq, LANES), F32)
        for _ in range(topk):
            mx = jnp.max(gate, axis=1, keepdims=True)
            idx = jnp.min(jnp.where(gate == mx, lane, LANES), axis=1, keepdims=True)
            hit = lane == idx
            sel = jnp.where(hit, 1.0, sel)
            gate = jnp.where(hit, NEG_INF, gate)
        m = jnp.maximum(m, jnp.max(jnp.where(sel > 0.0, bmax * SCALE, NEG_INF), axis=1, keepdims=True))

    p_own = jnp.exp(s_own - m)
    l0 = jnp.sum(p_own, axis=1, keepdims=True)
    vn = vn_ref[0]
    acc0 = p_own[:, 0:1] * vn[0:1, :]
    for j in range(1, t_new):
        acc0 = acc0 + p_own[:, j:j + 1] * vn[j:j + 1, :]

    if topk > 0:
        def pass2(n, carry):
            l, acc = carry
            s = s_scr[:, pl.ds(pl.multiple_of(n * blk, blk), blk)] * SCALE
            seln = jnp.max(jnp.where(lane == n, sel, 0.0), axis=1, keepdims=True)
            p = jnp.where(seln > 0.0, jnp.exp(s - m), 0.0)
            vb = vbuf[slot, pl.ds(n * ppb, ppb)].reshape(blk, MOBA_KV)
            return l + jnp.sum(p, axis=1, keepdims=True), acc + _mm(p, vb)

        l0, acc0 = lax.fori_loop(0, n_full, pass2, (l0, acc0))

    o_ref[0] = acc0 / l0


def _moba_sample(q_bd, kn8, vn8, pool_k, pool_v, page_table, layer, t_new):
    nseq, nq, _ = q_bd.shape
    n_pages = page_table.shape[1]
    ppb = MOBA_BLOCK // PAGE_SIZE
    assert n_pages % ppb == 0
    n_full = n_pages // ppb
    assert n_full <= LANES
    topk = min(MOBA_TOPK, n_full)
    grid_spec = pltpu.PrefetchScalarGridSpec(
        num_scalar_prefetch=1,
        grid=(nseq,),
        in_specs=[pl.BlockSpec((1, nq, MOBA_KV), lambda b, pt: (b, 0, 0)),
                  pl.BlockSpec((1, SUBLANES, MOBA_KV), lambda b, pt: (b, 0, 0)),
                  pl.BlockSpec((1, SUBLANES, MOBA_KV), lambda b, pt: (b, 0, 0)),
                  pl.BlockSpec(memory_space=pl.ANY),
                  pl.BlockSpec(memory_space=pl.ANY)],
        out_specs=pl.BlockSpec((1, nq, MOBA_KV), lambda b, pt: (b, 0, 0)),
        scratch_shapes=[pltpu.VMEM((2, n_pages, PAGE_SIZE, MOBA_KV), F32),
                        pltpu.VMEM((2, n_pages, PAGE_SIZE, MOBA_KV), F32),
                        pltpu.VMEM((nq, max(n_full, 1) * MOBA_BLOCK), F32),
                        pltpu.SemaphoreType.DMA((2, 2))])
    return pl.pallas_call(
        functools.partial(_moba_sample_kernel, layer, n_pages, topk, t_new),
        name="moba_sample",
        grid_spec=grid_spec,
        out_shape=jax.ShapeDtypeStruct((nseq, nq, MOBA_KV), F32),
        compiler_params=_params("arbitrary"),
    )(page_table.reshape(-1), q_bd, kn8, vn8, pool_k, pool_v)


def _rope_tables(pos):
    half = HEAD_DIM // 2
    inv = ROPE_THETA ** (-jnp.arange(half, dtype=F32) / half)
    ang = pos.astype(F32)[:, None] * inv[None, :]
    cos, sin = jnp.cos(ang), jnp.sin(ang)
    reps = LANES // HEAD_DIM
    cos_t = jnp.tile(jnp.concatenate([cos, cos], axis=1), (1, reps))
    sin_t = jnp.tile(jnp.concatenate([-sin, sin], axis=1), (1, reps))
    return cos_t, sin_t


EVEN_SEGS = ((0, SWA_Q + SWA_KV, True), (SWA_Q + SWA_KV, SWA_KV, False),
             (SWA_Q + 2 * SWA_KV, GDN_CONV_CH, False),
             (SWA_Q + 2 * SWA_KV + GDN_CONV_CH, GDN_V, False), (EVEN_MAIN, LANES, False))
ODD_SEGS = ((0, MOBA_Q, True), (MOBA_Q, MOBA_KV, True), (MOBA_Q + MOBA_KV, MOBA_KV, False))


def _pad_rows8(a, t_new):
    nseq = a.shape[0] // t_new
    a = a.reshape(nseq, t_new, a.shape[1])
    return jnp.pad(a, ((0, 0), (SUBLANES - t_new, 0), (0, 0)))


def _trunk(grp, x, batch, seq, wts, caches):
    (norm_g4, w1, w3, w2, w_even, w_out_even, sinks, conv_w, alog_p, dtb_p, gnorm, w_odd, w_out_odd,
     final_g) = wts
    depth = w1.shape[0]
    swa_k, swa_v, conv_out, gdn_out, moba_k, moba_v = [], [], [], [], [], []
    for l in range(depth):
        x = _ffn(grp, x, l, 0, 0, norm_g4, w1, w3, w2)
        if l % 2 == 0:
            e = l // 2
            qk, va, conv_in, gz, ab = _proj(grp, x, l, 3, norm_g4, w_even, e, EVEN_SEGS)
            if grp.prompt:
                oa = _swa_prompt(qk, va, sinks[e], batch, seq)
                ka = qk[:, SWA_Q:].reshape(batch, seq, SWA_KV_HEADS, HEAD_DIM)
                swa_k.append(ka[:, seq - WINDOW:])
                swa_v.append(va.reshape(batch, seq, SWA_KV_HEADS, HEAD_DIM)[:, seq - WINDOW:])
                ob, s_new = _gdn_prompt(conv_in, gz, ab, conv_w, alog_p, dtb_p, gnorm, e, batch, seq)
                conv_out.append(conv_in.reshape(batch, seq, GDN_CONV_CH)[:, seq - (CONV_WIDTH - 1):])
            else:
                ck, cv, conv_st, gdn_st = caches[:4]
                oa, nk, nv = _swa_sample(qk, va, ck, cv, sinks[e], e, batch, seq)
                swa_k.append(nk.reshape(batch, WINDOW, SWA_KV_HEADS, HEAD_DIM))
                swa_v.append(nv.reshape(batch, WINDOW, SWA_KV_HEADS, HEAD_DIM))
                hist = jnp.concatenate([conv_st[e], conv_in.reshape(batch, seq, GDN_CONV_CH)], axis=1)
                conv_out.append(hist[:, seq:])
                xx8 = jnp.pad(hist, ((0, 0), (SUBLANES - hist.shape[1], 0), (0, 0)))
                ob8, s_new = _gdn_sample(xx8, _pad_rows8(gz, seq), _pad_rows8(ab, seq), gdn_st, conv_w,
                                         alog_p, dtb_p, gnorm, e, batch, seq)
                ob = ob8[:, SUBLANES - seq:].reshape(batch * seq, GDN_V)
            gdn_out.append(s_new)
            x = _outproj(grp, x, l, 5, [oa, ob], w_out_even, e)
        else:
            j = l // 2
            q, k, v = _proj(grp, x, l, 3, norm_g4, w_odd, j, ODD_SEGS)
            moba_k.append(k.reshape(batch, seq, MOBA_KV_HEADS, HEAD_DIM))
            moba_v.append(v.reshape(batch, seq, MOBA_KV_HEADS, HEAD_DIM))
            if grp.prompt:
                vt = jnp.swapaxes(v.reshape(batch, seq, MOBA_KV), 1, 2)
                o = _moba_prompt(q, k, vt, batch, seq)
            else:
                pool_k, pool_v, page_table = caches[4:]
                q5 = q.reshape(batch, seq, MOBA_KV_HEADS, MOBA_GROUP, HEAD_DIM).transpose(0, 2, 1, 3, 4)
                eye = jnp.eye(MOBA_KV_HEADS, dtype=F32)
                q_bd = (q5[:, :, :, :, None, :] * eye[None, :, None, None, :, None]).reshape(
                    batch, MOBA_KV_HEADS * seq * MOBA_GROUP, MOBA_KV)
                kn8 = jnp.pad(k.reshape(batch, seq, MOBA_KV), ((0, 0), (0, SUBLANES - seq), (0, 0)))
                vn8 = jnp.pad(v.reshape(batch, seq, MOBA_KV), ((0, 0), (0, SUBLANES - seq), (0, 0)))
                o_bd = _moba_sample(q_bd, kn8, vn8, pool_k, pool_v, page_table, j, seq)
                o6 = o_bd.reshape(batch, MOBA_KV_HEADS, seq, MOBA_GROUP, MOBA_KV_HEADS, HEAD_DIM)
                idx = jnp.arange(MOBA_KV_HEADS)
                o = o6[:, idx, :, :, idx]
                o = o.transpose(1, 2, 0, 3, 4).reshape(batch * seq, MOBA_Q)
            x = _outproj(grp, x, l, 5, [o], w_out_odd, j)
        x = _ffn(grp, x, l, 1, 6, norm_g4, w1, w3, w2)
    y = _final_norm(grp, x, final_g)
    return (y, jnp.stack(swa_k), jnp.stack(swa_v), jnp.stack(conv_out), jnp.stack(gdn_out),
            jnp.stack(moba_k), jnp.stack(moba_v))


def kernel(x_prompt, x_sample, cache_swa_k, cache_swa_v, state_conv, state_gdn, cache_moba_k, cache_moba_v,
           page_table, c_prompt, c_sample, w_ada, b_ada, norm_g, ffn_w1, ffn_w3, ffn_w2, w_in_even,
           w_out_even, swa_sinks, conv_w, a_log, dt_bias, gdn_norm_g, w_in_odd, w_out_odd, final_norm_g):
    batch, seq, d = x_prompt.shape
    nseq, t_new, _ = x_sample.shape
    depth = w_ada.shape[0]
    past_len = page_table.shape[1] * PAGE_SIZE
    assert past_len % MOBA_BLOCK == 0 and seq % MOBA_BLOCK == 0 and seq % GDN_CHUNK == 0
    assert t_new <= SUBLANES and t_new >= CONV_WIDTH - 1

    bp = -(-batch // SUBLANES) * SUBLANES
    c_p = jnp.pad(c_prompt, ((0, bp - batch), (0, 0)))
    c_s = jnp.repeat(c_sample, t_new, axis=0)
    mod_p, mod_s = _adaln(c_p, c_s, w_ada, b_ada)
    mod_p = mod_p.reshape(depth, bp, 1, N_MOD * d)

    n_even = w_in_even.shape[0]
    w_even = jnp.pad(w_in_even.astype(BF16), ((0, 0), (0, 0), (0, EVEN_MAIN + LANES - w_in_even.shape[2])))
    gate_pad = ((0, 0), (0, 0), (0, LANES - GDN_HEADS))
    wts = (norm_g.reshape(depth, 3, 1, d), ffn_w1.astype(BF16), ffn_w3.astype(BF16), ffn_w2.astype(BF16),
           w_even, w_out_even.astype(BF16), swa_sinks, conv_w,
           jnp.pad(a_log.reshape(n_even, 1, GDN_HEADS), gate_pad),
           jnp.pad(dt_bias.reshape(n_even, 1, GDN_HEADS), gate_pad),
           gdn_norm_g.reshape(n_even, 1, GDN_DV), w_in_odd.astype(BF16), w_out_odd.astype(BF16),
           final_norm_g.reshape(1, d))

    cos_p, sin_p = _rope_tables(jnp.arange(seq, dtype=jnp.int32))
    grp_p = _Group(True, batch * seq, seq, mod_p, cos_p, sin_p)
    outs_p = _trunk(grp_p, x_prompt.reshape(batch * seq, d), batch, seq, wts, None)

    cos_s, sin_s = _rope_tables(past_len + jnp.arange(t_new, dtype=jnp.int32))
    grp_s = _Group(False, nseq * t_new, t_new, mod_s, jnp.tile(cos_s, (nseq, 1)), jnp.tile(sin_s, (nseq, 1)))
    n_phys = cache_moba_k.shape[1]
    caches = (cache_swa_k.reshape(n_even, nseq, WINDOW, SWA_KV), cache_swa_v.reshape(n_even, nseq, WINDOW, SWA_KV),
              state_conv, state_gdn,
              cache_moba_k.reshape(-1, n_phys, PAGE_SIZE, MOBA_KV), cache_moba_v.reshape(-1, n_phys, PAGE_SIZE, MOBA_KV),
              page_table)
    outs_s = _trunk(grp_s, x_sample.reshape(nseq * t_new, d), nseq, t_new, wts, caches)

    y_p = outs_p[0].reshape(batch, seq, d)
    y_s = outs_s[0].reshape(nseq, t_new, d)
    return (y_p, y_s) + tuple(outs_p[1:]) + tuple(outs_s[1:])
```

```python
import functools

import jax
import jax.numpy as jnp
from jax import lax
from jax.experimental import pallas as pl
from jax.experimental.pallas import tpu as pltpu

F32 = jnp.float32
BF16 = jnp.bfloat16

HEAD_DIM = 64
ROPE_THETA = 10000.0
NORM_EPS = 1e-6
SCALE = HEAD_DIM ** -0.5
HALF_STEP = 0.5
SWA_HEADS = 8
SWA_KV_HEADS = 2
SWA_GROUP = SWA_HEADS // SWA_KV_HEADS
WINDOW = 128
GDN_HEADS = 4
GDN_DK = 128
GDN_DV = 128
CONV_WIDTH = 4
GDN_CHUNK = 64
MOBA_HEADS = 16
MOBA_KV_HEADS = 4
MOBA_GROUP = MOBA_HEADS // MOBA_KV_HEADS
MOBA_BLOCK = 256
MOBA_TOPK = 3
PAGE_SIZE = 128
N_MOD = 9

SWA_Q = SWA_HEADS * HEAD_DIM
SWA_KV = SWA_KV_HEADS * HEAD_DIM
GDN_QK = GDN_HEADS * GDN_DK
GDN_V = GDN_HEADS * GDN_DV
GDN_CONV_CH = 2 * GDN_QK + GDN_V
EVEN_MAIN = SWA_Q + 2 * SWA_KV + GDN_CONV_CH + GDN_V
MOBA_Q = MOBA_HEADS * HEAD_DIM
MOBA_KV = MOBA_KV_HEADS * HEAD_DIM

LANES = 128
SUBLANES = 8
VMEM_LIMIT_BYTES = 56 * 1024 * 1024

NEG_INF = float("-inf")


def _params(*sem):
    return pltpu.CompilerParams(dimension_semantics=sem, vmem_limit_bytes=VMEM_LIMIT_BYTES)


def _mm(a, b):
    return jnp.dot(a.astype(BF16), b.astype(BF16), preferred_element_type=F32)


def _mm_nt(a, b):
    return lax.dot_general(a.astype(BF16), b.astype(BF16), (((1,), (1,)), ((), ())),
                           preferred_element_type=F32)


def _silu(x):
    return x * jax.nn.sigmoid(x)


def _rms(x, g):
    return x * lax.rsqrt(jnp.mean(x * x, axis=-1, keepdims=True) + NORM_EPS) * g


def _l2norm(x):
    return x * lax.rsqrt(jnp.sum(x * x, axis=-1, keepdims=True) + 1e-6)


def _softplus(x):
    return jnp.maximum(x, 0.0) + jnp.log(1.0 + jnp.exp(-jnp.abs(x)))


def _rope(y, cos, sin_signed):
    lane = lax.broadcasted_iota(jnp.int32, (y.shape[0], LANES), 1)
    first_half = (lane % HEAD_DIM) < (HEAD_DIM // 2)
    outs = []
    for c in range(y.shape[1] // LANES):
        yc = y[:, c * LANES:(c + 1) * LANES]
        swapped = jnp.where(first_half, pltpu.roll(yc, LANES - HEAD_DIM // 2, 1),
                            pltpu.roll(yc, HEAD_DIM // 2, 1))
        outs.append(yc * cos + swapped * sin_signed)
    return outs[0] if len(outs) == 1 else jnp.concatenate(outs, axis=1)


def _adaln_kernel(cp_ref, cs_ref, w_ref, b_ref, op_ref, os_ref):
    w = w_ref[...].astype(BF16)
    b = b_ref[...]
    op_ref[...] = jnp.dot(_silu(cp_ref[...]).astype(BF16), w, preferred_element_type=F32) + b
    os_ref[...] = jnp.dot(_silu(cs_ref[...]).astype(BF16), w, preferred_element_type=F32) + b


def _adaln(c_p, c_s, w_ada, b_ada):
    depth, d, n = w_ada.shape
    rp, rs = c_p.shape[0], c_s.shape[0]
    tn = d
    return pl.pallas_call(
        _adaln_kernel,
        name="adaln",
        grid=(depth, n // tn),
        in_specs=[pl.BlockSpec((rp, d), lambda l, j: (0, 0)),
                  pl.BlockSpec((rs, d), lambda l, j: (0, 0)),
                  pl.BlockSpec((None, d, tn), lambda l, j: (l, 0, j)),
                  pl.BlockSpec((None, 1, tn), lambda l, j: (l, 0, j))],
        out_specs=[pl.BlockSpec((None, rp, tn), lambda l, j: (l, 0, j)),
                   pl.BlockSpec((None, rs, tn), lambda l, j: (l, 0, j))],
        out_shape=[jax.ShapeDtypeStruct((depth, rp, n), F32),
                   jax.ShapeDtypeStruct((depth, rs, n), F32)],
        compiler_params=_params("arbitrary", "arbitrary"),
    )(c_p, c_s, w_ada, b_ada.reshape(depth, 1, n))


class _Group:
    def __init__(self, prompt, rows, seq, mod, cos, sin):
        self.prompt = prompt
        self.rows = rows
        self.seq = seq
        self.mod = mod
        self.cos = cos
        self.sin = sin

    def tile(self, want):
        limit = self.seq if self.prompt else self.rows
        t = min(want, limit)
        assert limit % t == 0 and t % SUBLANES == 0
        return t

    def mod_spec(self, layer, k, tm, d):
        if self.prompt:
            per_seq = self.seq // tm
            return pl.BlockSpec((None, None, 1, d), lambda i, *_: (layer, i // per_seq, 0, k))
        return pl.BlockSpec((None, tm, d), lambda i, *_: (layer, i, k))

    def rope_spec(self, tm):
        if self.prompt:
            per_seq = self.seq // tm
            return pl.BlockSpec((tm, LANES), lambda i, *_: (i % per_seq, 0))
        return pl.BlockSpec((tm, LANES), lambda i, *_: (i, 0))


def _ffn_kernel(tf, x_ref, sh_ref, sc_ref, gt_ref, g_ref, w1_ref, w3_ref, w2_ref, o_ref, h_scr, act_scr):
    h_scr[...] = (_rms(x_ref[...], g_ref[...]) * (1.0 + sc_ref[...]) + sh_ref[...]).astype(BF16)
    for j in range(w1_ref.shape[1] // tf):
        cols = slice(j * tf, (j + 1) * tf)
        a = jnp.dot(h_scr[...], w1_ref[:, cols], preferred_element_type=F32)
        b = jnp.dot(h_scr[...], w3_ref[:, cols], preferred_element_type=F32)
        act_scr[:, cols] = (_silu(a) * b).astype(BF16)
    y = jnp.dot(act_scr[...], w2_ref[...], preferred_element_type=F32)
    o_ref[...] = x_ref[...] + HALF_STEP * gt_ref[...] * y


def _ffn(grp, x, layer, which, kbase, norm_g, w1, w3, w2):
    m, d = x.shape
    f = w1.shape[-1]
    tm = grp.tile(1024)
    tf = 256 if f % 256 == 0 else LANES
    resident = pl.Buffered(1)
    wspec_in = pl.BlockSpec((None, None, d, f), lambda i: (layer, which, 0, 0), pipeline_mode=resident)
    return pl.pallas_call(
        functools.partial(_ffn_kernel, tf),
        name="ffn",
        grid=(m // tm,),
        in_specs=[pl.BlockSpec((tm, d), lambda i: (i, 0)),
                  grp.mod_spec(layer, kbase, tm, d),
                  grp.mod_spec(layer, kbase + 1, tm, d),
                  grp.mod_spec(layer, kbase + 2, tm, d),
                  pl.BlockSpec((None, None, 1, d), lambda i: (layer, 2 * which, 0, 0)),
                  wspec_in, wspec_in,
                  pl.BlockSpec((None, None, f, d), lambda i: (layer, which, 0, 0), pipeline_mode=resident)],
        out_specs=pl.BlockSpec((tm, d), lambda i: (i, 0)),
        out_shape=jax.ShapeDtypeStruct((m, d), F32),
        scratch_shapes=[pltpu.VMEM((tm, d), BF16), pltpu.VMEM((tm, f), BF16)],
        compiler_params=_params("parallel"),
    )(x, grp.mod, grp.mod, grp.mod, norm_g, w1, w3, w2)


def _proj_kernel(segs, x_ref, sh_ref, sc_ref, g_ref, w_ref, cos_ref, sin_ref, *out_refs):
    h = (_rms(x_ref[...], g_ref[...]) * (1.0 + sc_ref[...]) + sh_ref[...]).astype(BF16)
    for (start, width, rope), o_ref in zip(segs, out_refs):
        y = jnp.dot(h, w_ref[:, start:start + width], preferred_element_type=F32)
        if rope:
            y = _rope(y, cos_ref[...], sin_ref[...])
        o_ref[...] = y


def _proj(grp, x, layer, kbase, norm_g, w, widx, segs):
    m, d = x.shape
    n = w.shape[-1]
    tm = grp.tile(512)
    return pl.pallas_call(
        functools.partial(_proj_kernel, segs),
        name="mixer_in_proj",
        grid=(m // tm,),
        in_specs=[pl.BlockSpec((tm, d), lambda i: (i, 0)),
                  grp.mod_spec(layer, kbase, tm, d),
                  grp.mod_spec(layer, kbase + 1, tm, d),
                  pl.BlockSpec((None, None, 1, d), lambda i: (layer, 1, 0, 0)),
                  pl.BlockSpec((None, d, n), lambda i: (widx, 0, 0)),
                  grp.rope_spec(tm), grp.rope_spec(tm)],
        out_specs=[pl.BlockSpec((tm, width), lambda i: (i, 0)) for _, width, _ in segs],
        out_shape=[jax.ShapeDtypeStruct((m, width), F32) for _, width, _ in segs],
        compiler_params=_params("parallel"),
    )(x, grp.mod, grp.mod, norm_g, w, grp.cos, grp.sin)


def _outproj_kernel(n_in, x_ref, gt_ref, *refs):
    a_refs, w_refs, o_ref = refs[:n_in], refs[n_in:2 * n_in], refs[2 * n_in]
    acc = _mm(a_refs[0][...], w_refs[0][...])
    for a_ref, w_ref in zip(a_refs[1:], w_refs[1:]):
        acc = acc + _mm(a_ref[...], w_ref[...])
    o_ref[...] = x_ref[...] + gt_ref[...] * acc


def _outproj(grp, x, layer, kgate, acts, w, widx):
    m, d = x.shape
    tm = grp.tile(512)
    widths = [a.shape[1] for a in acts]
    assert sum(widths) == w.shape[1]
    blocks = [sum(widths[:i]) // widths[i] for i in range(len(acts))]
    return pl.pallas_call(
        functools.partial(_outproj_kernel, len(acts)),
        name="mixer_out_proj",
        grid=(m // tm,),
        in_specs=([pl.BlockSpec((tm, d), lambda i: (i, 0)), grp.mod_spec(layer, kgate, tm, d)]
                  + [pl.BlockSpec((tm, wd), lambda i: (i, 0)) for wd in widths]
                  + [pl.BlockSpec((None, wd, d), functools.partial(lambda blk, i: (widx, blk, 0), blk))
                     for wd, blk in zip(widths, blocks)]),
        out_specs=pl.BlockSpec((tm, d), lambda i: (i, 0)),
        out_shape=jax.ShapeDtypeStruct((m, d), F32),
        compiler_params=_params("parallel"),
    )(x, grp.mod, *acts, *([w] * len(acts)))


def _final_norm_kernel(x_ref, g_ref, o_ref):
    o_ref[...] = _rms(x_ref[...], g_ref[...])


def _final_norm(grp, x, g):
    m, d = x.shape
    tm = grp.tile(1024)
    return pl.pallas_call(
        _final_norm_kernel,
        name="final_norm",
        grid=(m // tm,),
        in_specs=[pl.BlockSpec((tm, d), lambda i: (i, 0)), pl.BlockSpec((1, d), lambda i: (0, 0))],
        out_specs=pl.BlockSpec((tm, d), lambda i: (i, 0)),
        out_shape=jax.ShapeDtypeStruct((m, d), F32),
        compiler_params=_params("parallel"),
    )(x, g)


def _sink_attention(s, valid, sink, v16):
    s = jnp.where(valid, s, NEG_INF)
    m = jnp.maximum(jnp.max(s, axis=-1, keepdims=True), sink)
    p = jnp.exp(s - m)
    den = jnp.sum(p, axis=-1, keepdims=True) + jnp.exp(sink - m)
    return jnp.dot(p.astype(BF16), v16, preferred_element_type=F32) / den


def _swa_prompt_kernel(sink_ref, qk_ref, kprev_ref, v_ref, vprev_ref, o_ref):
    n = pl.program_id(1)
    w = WINDOW
    qk = qk_ref[...]
    kcat = jnp.concatenate([kprev_ref[...], qk[:, SWA_Q:SWA_Q + SWA_KV]], axis=0)
    vcat = jnp.concatenate([vprev_ref[...], v_ref[...]], axis=0)
    qi = lax.broadcasted_iota(jnp.int32, (w, 2 * w), 0)
    kj = lax.broadcasted_iota(jnp.int32, (w, 2 * w), 1)
    valid = (kj <= qi + w) & (kj > qi) & ((kj >= w) | (n > 0))
    for h in range(SWA_KV_HEADS):
        k16 = kcat[:, h * HEAD_DIM:(h + 1) * HEAD_DIM].astype(BF16)
        v16 = vcat[:, h * HEAD_DIM:(h + 1) * HEAD_DIM].astype(BF16)
        for g in range(SWA_GROUP):
            hh = h * SWA_GROUP + g
            q = qk[:, hh * HEAD_DIM:(hh + 1) * HEAD_DIM]
            s = _mm_nt(q, k16) * SCALE
            o_ref[:, hh * HEAD_DIM:(hh + 1) * HEAD_DIM] = _sink_attention(s, valid, sink_ref[hh], v16)


def _swa_prompt(qk, v, sinks, batch, seq):
    nq = seq // WINDOW
    kblk = SWA_Q // SWA_KV

    def cur(b, n):
        return (b * nq + n, 0)

    def prev(b, n):
        return (b * nq + jnp.maximum(n - 1, 0), 0)

    return pl.pallas_call(
        _swa_prompt_kernel,
        name="swa_prompt",
        grid=(batch, nq),
        in_specs=[pl.BlockSpec(memory_space=pltpu.SMEM),
                  pl.BlockSpec((WINDOW, SWA_Q + SWA_KV), cur),
                  pl.BlockSpec((WINDOW, SWA_KV), lambda b, n: (b * nq + jnp.maximum(n - 1, 0), kblk)),
                  pl.BlockSpec((WINDOW, SWA_KV), cur),
                  pl.BlockSpec((WINDOW, SWA_KV), prev)],
        out_specs=pl.BlockSpec((WINDOW, SWA_Q), cur),
        out_shape=jax.ShapeDtypeStruct((batch * seq, SWA_Q), F32),
        compiler_params=_params("parallel", "parallel"),
    )(sinks, qk, qk, v, v)


def _swa_sample_kernel(t_new, sink_ref, qk_ref, v_ref, ck_ref, cv_ref, o_ref, nk_ref, nv_ref):
    per_tile = SUBLANES // t_new
    n_tiles = qk_ref.shape[0] // SUBLANES
    w = WINDOW
    nkeys = w + SUBLANES
    r8 = lax.broadcasted_iota(jnp.int32, (SUBLANES, SWA_KV), 0)
    rows = SWA_GROUP * SUBLANES
    rr = lax.broadcasted_iota(jnp.int32, (rows, nkeys), 0)
    jj = lax.broadcasted_iota(jnp.int32, (rows, nkeys), 1)
    r = rr % SUBLANES
    t = r % t_new
    rcol = lax.broadcasted_iota(jnp.int32, (rows, 1), 0)
    r8o = lax.broadcasted_iota(jnp.int32, (SUBLANES, HEAD_DIM), 0)
    for p in range(n_tiles):
        qk8 = qk_ref[p * SUBLANES:(p + 1) * SUBLANES, :]
        v8 = v_ref[p * SUBLANES:(p + 1) * SUBLANES, :]
        knew8 = qk8[:, SWA_Q:SWA_Q + SWA_KV]
        o_seq = []
        for s in range(per_tile):
            i = p * per_tile + s
            kb = ck_ref[i]
            vb = cv_ref[i]
            shift = (SUBLANES - t_new - s * t_new) % SUBLANES
            src_k = knew8 if shift == 0 else pltpu.roll(knew8, shift, 0)
            src_v = v8 if shift == 0 else pltpu.roll(v8, shift, 0)
            rk = pltpu.roll(kb, w - t_new, 0)
            rv = pltpu.roll(vb, w - t_new, 0)
            nk_ref[i, 0:w - SUBLANES, :] = rk[0:w - SUBLANES]
            nv_ref[i, 0:w - SUBLANES, :] = rv[0:w - SUBLANES]
            nk_ref[i, w - SUBLANES:w, :] = jnp.where(r8 >= SUBLANES - t_new, src_k, rk[w - SUBLANES:w])
            nv_ref[i, w - SUBLANES:w, :] = jnp.where(r8 >= SUBLANES - t_new, src_v, rv[w - SUBLANES:w])
            kext = jnp.concatenate([kb, knew8], axis=0)
            vext = jnp.concatenate([vb, v8], axis=0)
            jn = jj - w
            valid = ((jj < w) & (jj > t)) | ((jj >= w) & ((jn // t_new) == s) & ((jn % t_new) <= t))
            outs = []
            for h in range(SWA_KV_HEADS):
                k16 = kext[:, h * HEAD_DIM:(h + 1) * HEAD_DIM].astype(BF16)
                v16 = vext[:, h * HEAD_DIM:(h + 1) * HEAD_DIM].astype(BF16)
                qh = jnp.concatenate(
                    [qk8[:, (h * SWA_GROUP + g) * HEAD_DIM:(h * SWA_GROUP + g + 1) * HEAD_DIM]
                     for g in range(SWA_GROUP)], axis=0)
                sinkv = jnp.zeros((rows, 1), F32)
                for g in range(SWA_GROUP):
                    sinkv = jnp.where(rcol // SUBLANES == g, sink_ref[h * SWA_GROUP + g], sinkv)
                s_ = _mm_nt(qh, k16) * SCALE
                outs.append(_sink_attention(s_, valid, sinkv, v16))
            o_seq.append(outs)
        for h in range(SWA_KV_HEADS):
            for g in range(SWA_GROUP):
                tile = o_seq[0][h][g * SUBLANES:(g + 1) * SUBLANES]
                for s in range(1, per_tile):
                    tile = jnp.where(r8o // t_new == s, o_seq[s][h][g * SUBLANES:(g + 1) * SUBLANES], tile)
                hh = h * SWA_GROUP + g
                o_ref[p * SUBLANES:(p + 1) * SUBLANES, hh * HEAD_DIM:(hh + 1) * HEAD_DIM] = tile


def _swa_sample(qk, v, ck, cv, sinks, layer_e, nseq, t_new):
    assert SUBLANES % t_new == 0
    bb = SUBLANES
    while nseq % bb:
        bb //= 2
    assert (bb * t_new) % SUBLANES == 0
    rows = bb * t_new
    cache_spec = pl.BlockSpec((None, bb, WINDOW, SWA_KV), lambda i: (layer_e, i, 0, 0))
    out_cache = pl.BlockSpec((bb, WINDOW, SWA_KV), lambda i: (i, 0, 0))
    return pl.pallas_call(
        functools.partial(_swa_sample_kernel, t_new),
        name="swa_sample",
        grid=(nseq // bb,),
        in_specs=[pl.BlockSpec(memory_space=pltpu.SMEM),
                  pl.BlockSpec((rows, SWA_Q + SWA_KV), lambda i: (i, 0)),
                  pl.BlockSpec((rows, SWA_KV), lambda i: (i, 0)),
                  cache_spec, cache_spec],
        out_specs=[pl.BlockSpec((rows, SWA_Q), lambda i: (i, 0)), out_cache, out_cache],
        out_shape=[jax.ShapeDtypeStruct((nseq * t_new, SWA_Q), F32),
                   jax.ShapeDtypeStruct((nseq, WINDOW, SWA_KV), F32),
                   jax.ShapeDtypeStruct((nseq, WINDOW, SWA_KV), F32)],
        compiler_params=_params("parallel"),
    )(sinks, qk, v, ck, cv)


def _gdn_gates(ab, alog_row, dtb_row):
    g_all = -jnp.exp(alog_row) * _softplus(ab + dtb_row)
    beta_all = jax.nn.sigmoid(ab)
    return g_all, beta_all


def _cumsum_rows(x):
    n = x.shape[0]
    r = lax.broadcasted_iota(jnp.int32, x.shape, 0)
    s = 1
    while s < n:
        x = x + jnp.where(r >= s, pltpu.roll(x, s, 0), 0.0)
        s *= 2
    return x


GDN_INV_BASE = 8


def _unit_lower_inverse(lmat, ri, cj, eye, n):
    bsz = GDN_INV_BASE
    same = (ri // bsz) == (cj // bsz)
    pw = jnp.where(same, -lmat, 0.0)
    inv = eye + pw
    for _ in range((bsz - 1).bit_length() - 1):
        pw = _mm(pw, pw)
        inv = inv + _mm(inv, pw)
    while bsz < n:
        merged = (ri // (2 * bsz)) == (cj // (2 * bsz))
        off = jnp.where(merged & jnp.logical_not(same), lmat, 0.0)
        inv = inv - _mm(inv, _mm(off, inv))
        same = merged
        bsz *= 2
    return inv


def _gdn_prompt_kernel(x_ref, gz_ref, ab_ref, cw_ref, alog_ref, dtb_ref, gn_ref, o_ref, s_out_ref,
                       xs_scr, s_scr):
    c = pl.program_id(1)
    ch = GDN_CHUNK

    @pl.when(c == 0)
    def _():
        xs_scr[0:SUBLANES, :] = jnp.zeros((SUBLANES, GDN_CONV_CH), F32)
        s_scr[...] = jnp.zeros_like(s_scr)

    xs_scr[SUBLANES:SUBLANES + ch, :] = x_ref[...]
    cw = cw_ref[...]
    base = SUBLANES - (CONV_WIDTH - 1)
    conv = xs_scr[base:base + ch, :] * cw[0:1, :]
    for j in range(1, CONV_WIDTH):
        conv = conv + xs_scr[base + j:base + j + ch, :] * cw[j:j + 1, :]
    xs_scr[0:SUBLANES, :] = xs_scr[ch:ch + SUBLANES, :]
    qkv = _silu(conv)

    g_all, beta_all = _gdn_gates(ab_ref[...], alog_ref[...], dtb_ref[...])
    gcum = _cumsum_rows(g_all)

    nh = GDN_HEADS
    rows = nh * ch
    heads = range(nh)
    q_s = jnp.concatenate([_l2norm(qkv[:, h * GDN_DK:(h + 1) * GDN_DK]) * (GDN_DK ** -0.5) for h in heads], axis=0)
    k_s = jnp.concatenate([_l2norm(qkv[:, GDN_QK + h * GDN_DK:GDN_QK + (h + 1) * GDN_DK]) for h in heads], axis=0)
    v_s = jnp.concatenate([qkv[:, 2 * GDN_QK + h * GDN_DV:2 * GDN_QK + (h + 1) * GDN_DV] for h in heads], axis=0)
    gc = jnp.concatenate([gcum[:, h:h + 1] for h in heads], axis=0)
    beta = jnp.concatenate([beta_all[:, nh + h:nh + h + 1] for h in heads], axis=0)
    glast = jnp.concatenate([jnp.broadcast_to(gcum[ch - 1:ch, h:h + 1], (ch, 1)) for h in heads], axis=0)
    gr = jnp.broadcast_to(gc, (rows, LANES)).T[0:1, :]

    ri = lax.broadcasted_iota(jnp.int32, (rows, rows), 0)
    cj = lax.broadcasted_iota(jnp.int32, (rows, rows), 1)
    same_head = (ri // ch) == (cj // ch)
    eye = (ri == cj).astype(F32)
    decay = jnp.where(same_head & (cj <= ri), jnp.exp(gc - gr), 0.0)
    eg = jnp.exp(gc)
    kbeta = k_s * beta
    lmat = jnp.where(cj < ri, _mm_nt(kbeta, k_s) * decay, 0.0)
    inv = _unit_lower_inverse(lmat, ri, cj, eye, ch)
    wu = _mm(inv, jnp.concatenate([kbeta * eg, v_s * beta], axis=1))
    w_s = wu[:, :GDN_DK]
    u_s = wu[:, GDN_DK:]
    attn = _mm_nt(q_s, k_s) * decay

    hrow = lax.broadcasted_iota(jnp.int32, (rows, 1), 0) // ch
    hcol = lax.broadcasted_iota(jnp.int32, (1, rows), 1) // ch

    def by_head(a):
        return jnp.concatenate([jnp.where(hrow == j, a, 0.0) for j in heads], axis=1)

    s_prev = s_scr[...]
    ws_qs = _mm(jnp.concatenate([by_head(w_s), by_head(q_s * eg)], axis=0), s_prev)
    v_new = u_s - ws_qs[:rows]
    o_s = ws_qs[rows:] + _mm(attn, v_new)
    kdec_t = (k_s * jnp.exp(glast - gc)).T
    kdec_bd = jnp.concatenate([jnp.where(hcol == j, kdec_t, 0.0) for j in heads], axis=0)
    gl_rows = jnp.concatenate(
        [jnp.broadcast_to(jnp.exp(gcum[ch - 1:ch, h:h + 1]), (GDN_DK, 1)) for h in heads], axis=0)
    s_scr[...] = s_prev * gl_rows + _mm(kdec_bd, v_new)

    for h in heads:
        gz = gz_ref[:, h * GDN_DV:(h + 1) * GDN_DV]
        o_ref[:, h * GDN_DV:(h + 1) * GDN_DV] = _rms(o_s[h * ch:(h + 1) * ch], gn_ref[...]) * _silu(gz)

    @pl.when(c == pl.num_programs(1) - 1)
    def _():
        s_out_ref[...] = s_scr[...].reshape(nh, GDN_DK, GDN_DV)


def _gdn_prompt(conv_in, gz, ab, cw, alog, dtb, gn, layer_e, batch, seq):
    ch = GDN_CHUNK
    nch = seq // ch

    def rows(b, c):
        return (b * nch + c, 0)

    def par(shape):
        return pl.BlockSpec((None,) + shape, lambda b, c: (layer_e, 0, 0))

    return pl.pallas_call(
        _gdn_prompt_kernel,
        name="gdn_prompt",
        grid=(batch, nch),
        in_specs=[pl.BlockSpec((ch, GDN_CONV_CH), rows),
                  pl.BlockSpec((ch, GDN_V), rows),
                  pl.BlockSpec((ch, LANES), rows),
                  par((CONV_WIDTH, GDN_CONV_CH)), par((1, LANES)), par((1, LANES)), par((1, GDN_DV))],
        out_specs=[pl.BlockSpec((ch, GDN_V), rows),
                   pl.BlockSpec((None, GDN_HEADS, GDN_DK, GDN_DV), lambda b, c: (b, 0, 0, 0))],
        out_shape=[jax.ShapeDtypeStruct((batch * seq, GDN_V), F32),
                   jax.ShapeDtypeStruct((batch, GDN_HEADS, GDN_DK, GDN_DV), F32)],
        scratch_shapes=[pltpu.VMEM((SUBLANES + ch, GDN_CONV_CH), F32),
                        pltpu.VMEM((GDN_HEADS * GDN_DK, GDN_DV), F32)],
        compiler_params=_params("parallel", "arbitrary"),
    )(conv_in, gz, ab, cw, alog, dtb, gn)


def _gdn_sample_kernel(t_new, xx_ref, gz_ref, ab_ref, s_ref, cw_ref, alog_ref, dtb_ref, gn_ref,
                       o_ref, so_ref):
    first = SUBLANES - t_new
    cw = cw_ref[...]
    gn = gn_ref[...]

    def one(i, carry):
        xx = xx_ref[i]
        conv = xx * cw[CONV_WIDTH - 1:CONV_WIDTH, :]
        for d in range(1, CONV_WIDTH):
            conv = conv + pltpu.roll(xx, d, 0) * cw[CONV_WIDTH - 1 - d:CONV_WIDTH - d, :]
        qkv = _silu(conv)
        g_all, beta_all = _gdn_gates(ab_ref[i], alog_ref[...], dtb_ref[...])
        a_all = jnp.exp(g_all)
        gz = gz_ref[i]
        o_ref[i, 0:first, :] = jnp.zeros((first, GDN_V), F32)
        for h in range(GDN_HEADS):
            q = _l2norm(qkv[:, h * GDN_DK:(h + 1) * GDN_DK]) * (GDN_DK ** -0.5)
            k = _l2norm(qkv[:, GDN_QK + h * GDN_DK:GDN_QK + (h + 1) * GDN_DK])
            v = qkv[:, 2 * GDN_QK + h * GDN_DV:2 * GDN_QK + (h + 1) * GDN_DV]
            q_t = q.T
            k_t = k.T
            s = s_ref[i, h]
            for tok in range(t_new):
                r = first + tok
                kc = k_t[:, r:r + 1]
                qc = q_t[:, r:r + 1]
                a = a_all[r:r + 1, h:h + 1]
                b = beta_all[r:r + 1, GDN_HEADS + h:GDN_HEADS + h + 1]
                kts = jnp.sum(s * kc, axis=0, keepdims=True)
                s = a * s + kc * (b * (v[r:r + 1, :] - a * kts))
                o = jnp.sum(s * qc, axis=0, keepdims=True)
                o_ref[i, r:r + 1, h * GDN_DV:(h + 1) * GDN_DV] = (
                    _rms(o, gn) * _silu(gz[r:r + 1, h * GDN_DV:(h + 1) * GDN_DV]))
            so_ref[i, h] = s
        return carry

    lax.fori_loop(0, xx_ref.shape[0], one, 0)


def _gdn_sample(xx8, gz8, ab8, state, cw, alog, dtb, gn, layer_e, nseq, t_new):
    bb = 16
    while nseq % bb:
        bb //= 2

    def par(shape):
        return pl.BlockSpec((None,) + shape, lambda i: (layer_e, 0, 0))

    return pl.pallas_call(
        functools.partial(_gdn_sample_kernel, t_new),
        name="gdn_sample",
        grid=(nseq // bb,),
        in_specs=[pl.BlockSpec((bb, SUBLANES, GDN_CONV_CH), lambda i: (i, 0, 0)),
                  pl.BlockSpec((bb, SUBLANES, GDN_V), lambda i: (i, 0, 0)),
                  pl.BlockSpec((bb, SUBLANES, LANES), lambda i: (i, 0, 0)),
                  pl.BlockSpec((None, bb, GDN_HEADS, GDN_DK, GDN_DV), lambda i: (layer_e, i, 0, 0, 0)),
                  par((CONV_WIDTH, GDN_CONV_CH)), par((1, LANES)), par((1, LANES)), par((1, GDN_DV))],
        out_specs=[pl.BlockSpec((bb, SUBLANES, GDN_V), lambda i: (i, 0, 0)),
                   pl.BlockSpec((bb, GDN_HEADS, GDN_DK, GDN_DV), lambda i: (i, 0, 0, 0))],
        out_shape=[jax.ShapeDtypeStruct((nseq, SUBLANES, GDN_V), F32),
                   jax.ShapeDtypeStruct((nseq, GDN_HEADS, GDN_DK, GDN_DV), F32)],
        compiler_params=_params("parallel"),
    )(xx8, gz8, ab8, state, cw, alog, dtb, gn)


def _moba_prompt_kernel(nblk, topk, q_ref, k_ref, vt_ref, o_ref, kmean_scr, m_scr, l_scr, acc_scr):
    qt = pl.program_id(1)
    blk = MOBA_BLOCK
    tq = q_ref.shape[0]
    nq = MOBA_GROUP * tq
    nrow = kmean_scr.shape[0]

    @pl.when(qt == 0)
    def _():
        kmean_scr[...] = jnp.zeros_like(kmean_scr)
        for n in range(nblk):
            kmean_scr[n:n + 1, :] = jnp.mean(k_ref[n * blk:(n + 1) * blk, :], axis=0, keepdims=True)

    own = qt
    own_start = pl.multiple_of(own * blk, blk)
    nidx = lax.broadcasted_iota(jnp.int32, (nrow, nq), 0)
    kk = lax.broadcasted_iota(jnp.int32, (blk, nq), 0)
    qi = lax.broadcasted_iota(jnp.int32, (blk, nq), 1) % tq

    for h in range(MOBA_KV_HEADS):
        lanes = slice(h * HEAD_DIM, (h + 1) * HEAD_DIM)
        qh = jnp.concatenate(
            [q_ref[:, (h * MOBA_GROUP + g) * HEAD_DIM:(h * MOBA_GROUP + g + 1) * HEAD_DIM]
             for g in range(MOBA_GROUP)], axis=0)
        qh16 = qh.astype(BF16)

        s_t = _mm_nt(k_ref[pl.ds(own_start, blk), lanes], qh16) * SCALE
        s_t = jnp.where(kk <= qi, s_t, NEG_INF)
        m0 = jnp.max(s_t, axis=0, keepdims=True)
        p = jnp.exp(s_t - m0)
        m_scr[...] = m0
        l_scr[...] = jnp.sum(p, axis=0, keepdims=True)
        acc_scr[...] = _mm(vt_ref[lanes, pl.ds(own_start, blk)], p)

        if topk > 0:
            gate_t = _mm_nt(kmean_scr[:, lanes], qh)
            gm = jnp.where(nidx < own, gate_t, NEG_INF)
            sel_t = jnp.zeros((nrow, nq), F32)
            for n in range(nblk - 1):
                row = gm[n:n + 1, :]
                beats = jnp.where(gm > row, 1.0, jnp.where((gm == row) & (nidx < n), 1.0, 0.0))
                cnt = jnp.sum(beats, axis=0, keepdims=True)
                keep = jnp.where(cnt < topk, 1.0, 0.0) * jnp.where(n < own, 1.0, 0.0)
                sel_t = jnp.where(nidx == n, keep, sel_t)

            def past(n, carry):
                start = pl.multiple_of(n * blk, blk)
                sel_row = jnp.sum(jnp.where(nidx == n, sel_t, 0.0), axis=0, keepdims=True)
                s_n = _mm_nt(k_ref[pl.ds(start, blk), lanes], qh16) * SCALE
                s_n = jnp.where(sel_row > 0.0, s_n, NEG_INF)
                m_old = m_scr[...]
                m_new = jnp.maximum(m_old, jnp.max(s_n, axis=0, keepdims=True))
                alpha = jnp.exp(m_old - m_new)
                p_n = jnp.exp(s_n - m_new)
                l_scr[...] = alpha * l_scr[...] + jnp.sum(p_n, axis=0, keepdims=True)
                acc_scr[...] = alpha * acc_scr[...] + _mm(vt_ref[lanes, pl.ds(start, blk)], p_n)
                m_scr[...] = m_new
                return carry

            lax.fori_loop(0, own, past, 0)

        o_t = acc_scr[...] / l_scr[...]
        for pair in range(MOBA_GROUP // 2):
            two = jnp.concatenate([o_t[:, (2 * pair) * tq:(2 * pair + 1) * tq],
                                   o_t[:, (2 * pair + 1) * tq:(2 * pair + 2) * tq]], axis=0)
            col = h * MOBA_GROUP * HEAD_DIM + pair * LANES
            o_ref[:, col:col + LANES] = two.T


def _moba_prompt(q, k, vt, batch, seq):
    blk = MOBA_BLOCK
    assert seq % blk == 0
    nblk = seq // blk
    topk = min(MOBA_TOPK, nblk - 1)
    nrow = -(-nblk // SUBLANES) * SUBLANES
    nq = MOBA_GROUP * blk
    return pl.pallas_call(
        functools.partial(_moba_prompt_kernel, nblk, topk),
        name="moba_prompt",
        grid=(batch, nblk),
        in_specs=[pl.BlockSpec((blk, MOBA_Q), lambda b, t: (b * nblk + t, 0)),
                  pl.BlockSpec((seq, MOBA_KV), lambda b, t: (b, 0)),
                  pl.BlockSpec((None, MOBA_KV, seq), lambda b, t: (b, 0, 0))],
        out_specs=pl.BlockSpec((blk, MOBA_Q), lambda b, t: (b * nblk + t, 0)),
        out_shape=jax.ShapeDtypeStruct((batch * seq, MOBA_Q), F32),
        scratch_shapes=[pltpu.VMEM((nrow, MOBA_KV), F32), pltpu.VMEM((1, nq), F32),
                        pltpu.VMEM((1, nq), F32), pltpu.VMEM((HEAD_DIM, nq), F32)],
        compiler_params=_params("parallel", "arbitrary"),
    )(q, k, vt)


def _moba_sample_kernel(layer, n_pages, topk, t_new, pt_ref, q_ref, kn_ref, vn_ref, pk_ref, pv_ref,
                        o_ref, kbuf, vbuf, s_scr, sem):
    b = pl.program_id(0)
    nb = pl.num_programs(0)
    blk = MOBA_BLOCK
    ppb = blk // PAGE_SIZE
    n_full = n_pages // ppb
    nq = q_ref.shape[1]

    def page_copies(seq_idx, slot, p):
        pg = pt_ref[seq_idx * n_pages + p]
        return (pltpu.make_async_copy(pk_ref.at[layer, pg], kbuf.at[slot, p], sem.at[0, slot]),
                pltpu.make_async_copy(pv_ref.at[layer, pg], vbuf.at[slot, p], sem.at[1, slot]))

    def fetch(seq_idx, slot):
        for p in range(n_pages):
            ck, cv = page_copies(seq_idx, slot, p)
            ck.start()
            cv.start()

    slot = b % 2

    @pl.when(b == 0)
    def _():
        fetch(0, 0)

    @pl.when(b + 1 < nb)
    def _():
        fetch(b + 1, 1 - slot)

    for p in range(n_pages):
        ck, cv = page_copies(b, slot, p)
        ck.wait()
        cv.wait()

    q16 = q_ref[0].astype(BF16)
    lane = lax.broadcasted_iota(jnp.int32, (nq, LANES), 1)

    s_own = _mm_nt(q16, kn_ref[0]) * SCALE
    rr = lax.broadcasted_iota(jnp.int32, s_own.shape, 0)
    jj = lax.broadcasted_iota(jnp.int32, s_own.shape, 1)
    tq = (rr % (t_new * MOBA_GROUP)) // MOBA_GROUP
    s_own = jnp.where((jj <= tq) & (jj < t_new), s_own, NEG_INF)
    m = jnp.max(s_own, axis=1, keepdims=True)

    if topk > 0:
        for p in range(n_pages):
            s_scr[:, p * PAGE_SIZE:(p + 1) * PAGE_SIZE] = _mm(q16, kbuf[slot, p])
        gate = jnp.full((nq, LANES), NEG_INF, F32)
        bmax = gate
        for n in range(n_full):
            s = s_scr[:, n * blk:(n + 1) * blk]
            gate = jnp.where(lane == n, jnp.sum(s, axis=1, keepdims=True) * (1.0 / blk), gate)
            bmax = jnp.where(lane == n, jnp.max(s, axis=1, keepdims=True), bmax)
        sel = jnp.zeros((nq, LANES), F32)
        for _ in range(topk):
            mx = jnp.max(gate, axis=1, keepdims=True)
            idx = jnp.min(jnp.where(gate == mx, lane, LANES), axis=1, keepdims=True)
            hit = lane == idx
            sel = jnp.where(hit, 1.0, sel)
            gate = jnp.where(hit, NEG_INF, gate)
        m = jnp.maximum(m, jnp.max(jnp.where(sel > 0.0, bmax * SCALE, NEG_INF), axis=1, keepdims=True))

    p_own = jnp.exp(s_own - m)
    l0 = jnp.sum(p_own, axis=1, keepdims=True)
    vn = vn_ref[0]
    acc0 = p_own[:, 0:1] * vn[0:1, :]
    for j in range(1, t_new):
        acc0 = acc0 + p_own[:, j:j + 1] * vn[j:j + 1, :]

    if topk > 0:
        psum = jnp.zeros((nq, PAGE_SIZE), F32)
        for n in range(n_full):
            chosen = sel[:, n:n + 1] > 0.0
            for pg in range(n * ppb, (n + 1) * ppb):
                s = s_scr[:, pg * PAGE_SIZE:(pg + 1) * PAGE_SIZE] * SCALE
                p = jnp.where(chosen, jnp.exp(s - m), 0.0)
                psum = psum + p
                acc0 = acc0 + _mm_nt(p, vbuf[slot, pg])
        l0 = l0 + jnp.sum(psum, axis=1, keepdims=True)

    o_ref[0] = acc0 / l0


def _moba_sample(q_bd, kn8, vn8, pool_k, pool_v, page_table, layer, t_new):
    nseq, nq, _ = q_bd.shape
    n_pages = page_table.shape[1]
    ppb = MOBA_BLOCK // PAGE_SIZE
    assert n_pages % ppb == 0
    n_full = n_pages // ppb
    assert n_full <= LANES
    topk = min(MOBA_TOPK, n_full)
    grid_spec = pltpu.PrefetchScalarGridSpec(
        num_scalar_prefetch=1,
        grid=(nseq,),
        in_specs=[pl.BlockSpec((1, nq, MOBA_KV), lambda b, pt: (b, 0, 0)),
                  pl.BlockSpec((1, SUBLANES, MOBA_KV), lambda b, pt: (b, 0, 0)),
                  pl.BlockSpec((1, SUBLANES, MOBA_KV), lambda b, pt: (b, 0, 0)),
                  pl.BlockSpec(memory_space=pl.ANY),
                  pl.BlockSpec(memory_space=pl.ANY)],
        out_specs=pl.BlockSpec((1, nq, MOBA_KV), lambda b, pt: (b, 0, 0)),
        scratch_shapes=[pltpu.VMEM((2, n_pages, MOBA_KV, PAGE_SIZE), F32),
                        pltpu.VMEM((2, n_pages, MOBA_KV, PAGE_SIZE), F32),
                        pltpu.VMEM((nq, max(n_full, 1) * MOBA_BLOCK), F32),
                        pltpu.SemaphoreType.DMA((2, 2))])
    return pl.pallas_call(
        functools.partial(_moba_sample_kernel, layer, n_pages, topk, t_new),
        name="moba_sample",
        grid_spec=grid_spec,
        out_shape=jax.ShapeDtypeStruct((nseq, nq, MOBA_KV), F32),
        compiler_params=_params("arbitrary"),
    )(page_table.reshape(-1), q_bd, kn8, vn8, pool_k, pool_v)


def _rope_tables(pos):
    half = HEAD_DIM // 2
    inv = ROPE_THETA ** (-jnp.arange(half, dtype=F32) / half)
    ang = pos.astype(F32)[:, None] * inv[None, :]
    cos, sin = jnp.cos(ang), jnp.sin(ang)
    reps = LANES // HEAD_DIM
    cos_t = jnp.tile(jnp.concatenate([cos, cos], axis=1), (1, reps))
    sin_t = jnp.tile(jnp.concatenate([-sin, sin], axis=1), (1, reps))
    return cos_t, sin_t


EVEN_SEGS = ((0, SWA_Q + SWA_KV, True), (SWA_Q + SWA_KV, SWA_KV, False),
             (SWA_Q + 2 * SWA_KV, GDN_CONV_CH, False),
             (SWA_Q + 2 * SWA_KV + GDN_CONV_CH, GDN_V, False), (EVEN_MAIN, LANES, False))
ODD_SEGS = ((0, MOBA_Q, True), (MOBA_Q, MOBA_KV, True), (MOBA_Q + MOBA_KV, MOBA_KV, False))


def _pages_t(pool):
    n_layers, n_phys = pool.shape[:2]
    return jnp.transpose(pool, (0, 1, 3, 4, 2)).reshape(n_layers, n_phys, MOBA_KV, PAGE_SIZE)


def _pad_rows8(a, t_new):
    nseq = a.shape[0] // t_new
    a = a.reshape(nseq, t_new, a.shape[1])
    return jnp.pad(a, ((0, 0), (SUBLANES - t_new, 0), (0, 0)))


def _trunk(grp, x, batch, seq, wts, caches):
    (norm_g4, w1, w3, w2, w_even, w_out_even, sinks, conv_w, alog_p, dtb_p, gnorm, w_odd, w_out_odd,
     final_g) = wts
    depth = w1.shape[0]
    swa_k, swa_v, conv_out, gdn_out, moba_k, moba_v = [], [], [], [], [], []
    for l in range(depth):
        x = _ffn(grp, x, l, 0, 0, norm_g4, w1, w3, w2)
        if l % 2 == 0:
            e = l // 2
            qk, va, conv_in, gz, ab = _proj(grp, x, l, 3, norm_g4, w_even, e, EVEN_SEGS)
            if grp.prompt:
                oa = _swa_prompt(qk, va, sinks[e], batch, seq)
                ka = qk[:, SWA_Q:].reshape(batch, seq, SWA_KV_HEADS, HEAD_DIM)
                swa_k.append(ka[:, seq - WINDOW:])
                swa_v.append(va.reshape(batch, seq, SWA_KV_HEADS, HEAD_DIM)[:, seq - WINDOW:])
                ob, s_new = _gdn_prompt(conv_in, gz, ab, conv_w, alog_p, dtb_p, gnorm, e, batch, seq)
                conv_out.append(conv_in.reshape(batch, seq, GDN_CONV_CH)[:, seq - (CONV_WIDTH - 1):])
            else:
                ck, cv, conv_st, gdn_st = caches[:4]
                oa, nk, nv = _swa_sample(qk, va, ck, cv, sinks[e], e, batch, seq)
                swa_k.append(nk.reshape(batch, WINDOW, SWA_KV_HEADS, HEAD_DIM))
                swa_v.append(nv.reshape(batch, WINDOW, SWA_KV_HEADS, HEAD_DIM))
                hist = jnp.concatenate([conv_st[e], conv_in.reshape(batch, seq, GDN_CONV_CH)], axis=1)
                conv_out.append(hist[:, seq:])
                xx8 = jnp.pad(hist, ((0, 0), (SUBLANES - hist.shape[1], 0), (0, 0)))
                ob8, s_new = _gdn_sample(xx8, _pad_rows8(gz, seq), _pad_rows8(ab, seq), gdn_st, conv_w,
                                         alog_p, dtb_p, gnorm, e, batch, seq)
                ob = ob8[:, SUBLANES - seq:].reshape(batch * seq, GDN_V)
            gdn_out.append(s_new)
            x = _outproj(grp, x, l, 5, [oa, ob], w_out_even, e)
        else:
            j = l // 2
            q, k, v = _proj(grp, x, l, 3, norm_g4, w_odd, j, ODD_SEGS)
            moba_k.append(k.reshape(batch, seq, MOBA_KV_HEADS, HEAD_DIM))
            moba_v.append(v.reshape(batch, seq, MOBA_KV_HEADS, HEAD_DIM))
            if grp.prompt:
                vt = jnp.swapaxes(v.reshape(batch, seq, MOBA_KV), 1, 2)
                o = _moba_prompt(q, k, vt, batch, seq)
            else:
                pool_k, pool_v, page_table = caches[4:]
                q5 = q.reshape(batch, seq, MOBA_KV_HEADS, MOBA_GROUP, HEAD_DIM).transpose(0, 2, 1, 3, 4)
                eye = jnp.eye(MOBA_KV_HEADS, dtype=F32)
                q_bd = (q5[:, :, :, :, None, :] * eye[None, :, None, None, :, None]).reshape(
                    batch, MOBA_KV_HEADS * seq * MOBA_GROUP, MOBA_KV)
                kn8 = jnp.pad(k.reshape(batch, seq, MOBA_KV), ((0, 0), (0, SUBLANES - seq), (0, 0)))
                vn8 = jnp.pad(v.reshape(batch, seq, MOBA_KV), ((0, 0), (0, SUBLANES - seq), (0, 0)))
                o_bd = _moba_sample(q_bd, kn8, vn8, pool_k, pool_v, page_table, j, seq)
                o6 = o_bd.reshape(batch, MOBA_KV_HEADS, seq, MOBA_GROUP, MOBA_KV_HEADS, HEAD_DIM)
                idx = jnp.arange(MOBA_KV_HEADS)
                o = o6[:, idx, :, :, idx]
                o = o.transpose(1, 2, 0, 3, 4).reshape(batch * seq, MOBA_Q)
            x = _outproj(grp, x, l, 5, [o], w_out_odd, j)
        x = _ffn(grp, x, l, 1, 6, norm_g4, w1, w3, w2)
    y = _final_norm(grp, x, final_g)
    return (y, jnp.stack(swa_k), jnp.stack(swa_v), jnp.stack(conv_out), jnp.stack(gdn_out),
            jnp.stack(moba_k), jnp.stack(moba_v))


def kernel(x_prompt, x_sample, cache_swa_k, cache_swa_v, state_conv, state_gdn, cache_moba_k, cache_moba_v,
           page_table, c_prompt, c_sample, w_ada, b_ada, norm_g, ffn_w1, ffn_w3, ffn_w2, w_in_even,
           w_out_even, swa_sinks, conv_w, a_log, dt_bias, gdn_norm_g, w_in_odd, w_out_odd, final_norm_g):
    batch, seq, d = x_prompt.shape
    nseq, t_new, _ = x_sample.shape
    depth = w_ada.shape[0]
    past_len = page_table.shape[1] * PAGE_SIZE
    assert past_len % MOBA_BLOCK == 0 and seq % MOBA_BLOCK == 0 and seq % GDN_CHUNK == 0
    assert t_new <= SUBLANES and t_new >= CONV_WIDTH - 1

    bp = -(-batch // SUBLANES) * SUBLANES
    c_p = jnp.pad(c_prompt, ((0, bp - batch), (0, 0)))
    c_s = jnp.repeat(c_sample, t_new, axis=0)
    mod_p, mod_s = _adaln(c_p, c_s, w_ada, b_ada)
    mod_p = mod_p.reshape(depth, bp, 1, N_MOD * d)

    n_even = w_in_even.shape[0]
    w_even = jnp.pad(w_in_even.astype(BF16), ((0, 0), (0, 0), (0, EVEN_MAIN + LANES - w_in_even.shape[2])))
    gate_pad = ((0, 0), (0, 0), (0, LANES - GDN_HEADS))
    wts = (norm_g.reshape(depth, 3, 1, d), ffn_w1.astype(BF16), ffn_w3.astype(BF16), ffn_w2.astype(BF16),
           w_even, w_out_even.astype(BF16), swa_sinks, conv_w,
           jnp.pad(a_log.reshape(n_even, 1, GDN_HEADS), gate_pad),
           jnp.pad(dt_bias.reshape(n_even, 1, GDN_HEADS), gate_pad),
           gdn_norm_g.reshape(n_even, 1, GDN_DV), w_in_odd.astype(BF16), w_out_odd.astype(BF16),
           final_norm_g.reshape(1, d))

    cos_p, sin_p = _rope_tables(jnp.arange(seq, dtype=jnp.int32))
    grp_p = _Group(True, batch * seq, seq, mod_p, cos_p, sin_p)
    outs_p = _trunk(grp_p, x_prompt.reshape(batch * seq, d), batch, seq, wts, None)

    cos_s, sin_s = _rope_tables(past_len + jnp.arange(t_new, dtype=jnp.int32))
    grp_s = _Group(False, nseq * t_new, t_new, mod_s, jnp.tile(cos_s, (nseq, 1)), jnp.tile(sin_s, (nseq, 1)))
    n_phys = cache_moba_k.shape[1]
    caches = (cache_swa_k.reshape(n_even, nseq, WINDOW, SWA_KV), cache_swa_v.reshape(n_even, nseq, WINDOW, SWA_KV),
              state_conv, state_gdn,
              _pages_t(cache_moba_k), _pages_t(cache_moba_v), page_table)
    outs_s = _trunk(grp_s, x_sample.reshape(nseq * t_new, d), nseq, t_new, wts, caches)

    y_p = outs_p[0].reshape(batch, seq, d)
    y_s = outs_s[0].reshape(nseq, t_new, d)
    return (y_p, y_s) + tuple(outs_p[1:]) + tuple(outs_s[1:])
```

```python
import functools

import jax
import jax.numpy as jnp
from jax import lax
from jax.experimental import pallas as pl
from jax.experimental.pallas import tpu as pltpu

F32 = jnp.float32
BF16 = jnp.bfloat16

HEAD_DIM = 64
ROPE_THETA = 10000.0
NORM_EPS = 1e-6
SCALE = HEAD_DIM ** -0.5
LOG2_E = 1.4426950408889634
HALF_STEP = 0.5
SWA_HEADS = 8
SWA_KV_HEADS = 2
SWA_GROUP = SWA_HEADS // SWA_KV_HEADS
WINDOW = 128
GDN_HEADS = 4
GDN_DK = 128
GDN_DV = 128
CONV_WIDTH = 4
GDN_CHUNK = 64
MOBA_HEADS = 16
MOBA_KV_HEADS = 4
MOBA_GROUP = MOBA_HEADS // MOBA_KV_HEADS
MOBA_BLOCK = 256
MOBA_TOPK = 3
PAGE_SIZE = 128
N_MOD = 9

SWA_Q = SWA_HEADS * HEAD_DIM
SWA_KV = SWA_KV_HEADS * HEAD_DIM
GDN_QK = GDN_HEADS * GDN_DK
GDN_V = GDN_HEADS * GDN_DV
GDN_CONV_CH = 2 * GDN_QK + GDN_V
EVEN_MAIN = SWA_Q + 2 * SWA_KV + GDN_CONV_CH + GDN_V
MOBA_Q = MOBA_HEADS * HEAD_DIM
MOBA_KV = MOBA_KV_HEADS * HEAD_DIM

LANES = 128
SUBLANES = 8
VMEM_LIMIT_BYTES = 56 * 1024 * 1024

NEG_INF = float("-inf")


def _params(*sem):
    return pltpu.CompilerParams(dimension_semantics=sem, vmem_limit_bytes=VMEM_LIMIT_BYTES)


def _mm(a, b):
    return jnp.dot(a.astype(BF16), b.astype(BF16), preferred_element_type=F32)


def _mm_nt(a, b):
    return lax.dot_general(a.astype(BF16), b.astype(BF16), (((1,), (1,)), ((), ())),
                           preferred_element_type=F32)


def _silu(x):
    return x * jax.nn.sigmoid(x)


def _rms(x, g):
    return x * lax.rsqrt(jnp.mean(x * x, axis=-1, keepdims=True) + NORM_EPS) * g


def _l2norm(x):
    return x * lax.rsqrt(jnp.sum(x * x, axis=-1, keepdims=True) + 1e-6)


def _softplus(x):
    return jnp.maximum(x, 0.0) + jnp.log(1.0 + jnp.exp(-jnp.abs(x)))


def _rope(y, cos, sin_signed):
    lane = lax.broadcasted_iota(jnp.int32, (y.shape[0], LANES), 1)
    first_half = (lane % HEAD_DIM) < (HEAD_DIM // 2)
    outs = []
    for c in range(y.shape[1] // LANES):
        yc = y[:, c * LANES:(c + 1) * LANES]
        swapped = jnp.where(first_half, pltpu.roll(yc, LANES - HEAD_DIM // 2, 1),
                            pltpu.roll(yc, HEAD_DIM // 2, 1))
        outs.append(yc * cos + swapped * sin_signed)
    return outs[0] if len(outs) == 1 else jnp.concatenate(outs, axis=1)


def _adaln_kernel(cp_ref, cs_ref, w_ref, b_ref, op_ref, os_ref):
    w = w_ref[...].astype(BF16)
    b = b_ref[...]
    op_ref[...] = jnp.dot(_silu(cp_ref[...]).astype(BF16), w, preferred_element_type=F32) + b
    os_ref[...] = jnp.dot(_silu(cs_ref[...]).astype(BF16), w, preferred_element_type=F32) + b


def _adaln(c_p, c_s, w_ada, b_ada):
    depth, d, n = w_ada.shape
    rp, rs = c_p.shape[0], c_s.shape[0]
    tn = d
    return pl.pallas_call(
        _adaln_kernel,
        name="adaln",
        grid=(depth, n // tn),
        in_specs=[pl.BlockSpec((rp, d), lambda l, j: (0, 0)),
                  pl.BlockSpec((rs, d), lambda l, j: (0, 0)),
                  pl.BlockSpec((None, d, tn), lambda l, j: (l, 0, j)),
                  pl.BlockSpec((None, 1, tn), lambda l, j: (l, 0, j))],
        out_specs=[pl.BlockSpec((None, rp, tn), lambda l, j: (l, 0, j)),
                   pl.BlockSpec((None, rs, tn), lambda l, j: (l, 0, j))],
        out_shape=[jax.ShapeDtypeStruct((depth, rp, n), F32),
                   jax.ShapeDtypeStruct((depth, rs, n), F32)],
        compiler_params=_params("arbitrary", "arbitrary"),
    )(c_p, c_s, w_ada, b_ada.reshape(depth, 1, n))


class _Group:
    def __init__(self, prompt, rows, seq, mod, cos, sin):
        self.prompt = prompt
        self.rows = rows
        self.seq = seq
        self.mod = mod
        self.cos = cos
        self.sin = sin

    def tile(self, want):
        limit = self.seq if self.prompt else self.rows
        t = min(want, limit)
        assert limit % t == 0 and t % SUBLANES == 0
        return t

    def mod_spec(self, layer, k, tm, d):
        if self.prompt:
            per_seq = self.seq // tm
            return pl.BlockSpec((None, None, 1, d), lambda i, *_: (layer, i // per_seq, 0, k))
        return pl.BlockSpec((None, tm, d), lambda i, *_: (layer, i, k))

    def rope_spec(self, tm):
        if self.prompt:
            per_seq = self.seq // tm
            return pl.BlockSpec((tm, LANES), lambda i, *_: (i % per_seq, 0))
        return pl.BlockSpec((tm, LANES), lambda i, *_: (i, 0))


def _ffn_kernel(tf, x_ref, sh_ref, sc_ref, gt_ref, g_ref, w1_ref, w3_ref, w2_ref, fg_ref, o_ref, h_scr, act_scr):
    h_scr[...] = (_rms(x_ref[...], g_ref[...]) * (1.0 + sc_ref[...]) + sh_ref[...]).astype(BF16)
    for j in range(w1_ref.shape[1] // tf):
        cols = slice(j * tf, (j + 1) * tf)
        a = jnp.dot(h_scr[...], w1_ref[:, cols], preferred_element_type=F32)
        b = jnp.dot(h_scr[...], w3_ref[:, cols], preferred_element_type=F32)
        act_scr[:, cols] = (_silu(a) * b).astype(BF16)
    y = jnp.dot(act_scr[...], w2_ref[...], preferred_element_type=F32)
    out = x_ref[...] + HALF_STEP * gt_ref[...] * y
    o_ref[...] = out if fg_ref is None else _rms(out, fg_ref[...])


def _ffn_plain_kernel(tf, *refs):
    n_in = 8
    _ffn_kernel(tf, *refs[:n_in], None, *refs[n_in:])


def _ffn(grp, x, layer, which, kbase, norm_g, w1, w3, w2, final_g=None):
    m, d = x.shape
    f = w1.shape[-1]
    tm = grp.tile(1024)
    tf = 256 if f % 256 == 0 else LANES
    resident = pl.Buffered(1)
    wspec_in = pl.BlockSpec((None, None, d, f), lambda i: (layer, which, 0, 0), pipeline_mode=resident)
    in_specs = [pl.BlockSpec((tm, d), lambda i: (i, 0)),
                grp.mod_spec(layer, kbase, tm, d),
                grp.mod_spec(layer, kbase + 1, tm, d),
                grp.mod_spec(layer, kbase + 2, tm, d),
                pl.BlockSpec((None, None, 1, d), lambda i: (layer, 2 * which, 0, 0)),
                wspec_in, wspec_in,
                pl.BlockSpec((None, None, f, d), lambda i: (layer, which, 0, 0), pipeline_mode=resident)]
    args = [x, grp.mod, grp.mod, grp.mod, norm_g, w1, w3, w2]
    body = _ffn_plain_kernel
    if final_g is not None:
        in_specs.append(pl.BlockSpec((1, d), lambda i: (0, 0)))
        args.append(final_g)
        body = _ffn_kernel
    return pl.pallas_call(
        functools.partial(body, tf),
        name="ffn",
        grid=(m // tm,),
        in_specs=in_specs,
        out_specs=pl.BlockSpec((tm, d), lambda i: (i, 0)),
        out_shape=jax.ShapeDtypeStruct((m, d), F32),
        scratch_shapes=[pltpu.VMEM((tm, d), BF16), pltpu.VMEM((tm, f), BF16)],
        compiler_params=_params("parallel"),
    )(*args)


def _proj_kernel(segs, x_ref, sh_ref, sc_ref, g_ref, w_ref, cos_ref, sin_ref, *out_refs):
    h = (_rms(x_ref[...], g_ref[...]) * (1.0 + sc_ref[...]) + sh_ref[...]).astype(BF16)
    for (start, width, rope), o_ref in zip(segs, out_refs):
        y = jnp.dot(h, w_ref[:, start:start + width], preferred_element_type=F32)
        if rope:
            y = _rope(y, cos_ref[...], sin_ref[...])
        o_ref[...] = y


def _proj(grp, x, layer, kbase, norm_g, w, widx, segs):
    m, d = x.shape
    n = w.shape[-1]
    tm = grp.tile(512)
    return pl.pallas_call(
        functools.partial(_proj_kernel, segs),
        name="mixer_in_proj",
        grid=(m // tm,),
        in_specs=[pl.BlockSpec((tm, d), lambda i: (i, 0)),
                  grp.mod_spec(layer, kbase, tm, d),
                  grp.mod_spec(layer, kbase + 1, tm, d),
                  pl.BlockSpec((None, None, 1, d), lambda i: (layer, 1, 0, 0)),
                  pl.BlockSpec((None, d, n), lambda i: (widx, 0, 0)),
                  grp.rope_spec(tm), grp.rope_spec(tm)],
        out_specs=[pl.BlockSpec((tm, width), lambda i: (i, 0)) for _, width, _ in segs],
        out_shape=[jax.ShapeDtypeStruct((m, width), F32) for _, width, _ in segs],
        compiler_params=_params("parallel"),
    )(x, grp.mod, grp.mod, norm_g, w, grp.cos, grp.sin)


def _outproj_kernel(n_in, x_ref, gt_ref, *refs):
    a_refs, w_refs, o_ref = refs[:n_in], refs[n_in:2 * n_in], refs[2 * n_in]
    acc = _mm(a_refs[0][...], w_refs[0][...])
    for a_ref, w_ref in zip(a_refs[1:], w_refs[1:]):
        acc = acc + _mm(a_ref[...], w_ref[...])
    o_ref[...] = x_ref[...] + gt_ref[...] * acc


def _outproj(grp, x, layer, kgate, acts, w, widx):
    m, d = x.shape
    tm = grp.tile(512)
    widths = [a.shape[1] for a in acts]
    assert sum(widths) == w.shape[1]
    blocks = [sum(widths[:i]) // widths[i] for i in range(len(acts))]
    return pl.pallas_call(
        functools.partial(_outproj_kernel, len(acts)),
        name="mixer_out_proj",
        grid=(m // tm,),
        in_specs=([pl.BlockSpec((tm, d), lambda i: (i, 0)), grp.mod_spec(layer, kgate, tm, d)]
                  + [pl.BlockSpec((tm, wd), lambda i: (i, 0)) for wd in widths]
                  + [pl.BlockSpec((None, wd, d), functools.partial(lambda blk, i: (widx, blk, 0), blk))
                     for wd, blk in zip(widths, blocks)]),
        out_specs=pl.BlockSpec((tm, d), lambda i: (i, 0)),
        out_shape=jax.ShapeDtypeStruct((m, d), F32),
        compiler_params=_params("parallel"),
    )(x, grp.mod, *acts, *([w] * len(acts)))


def _sink_attention(s, valid, sink, v16):
    s = jnp.where(valid, s, NEG_INF)
    m = jnp.maximum(jnp.max(s, axis=-1, keepdims=True), sink)
    p = jnp.exp(s - m)
    den = jnp.sum(p, axis=-1, keepdims=True) + jnp.exp(sink - m)
    return jnp.dot(p.astype(BF16), v16, preferred_element_type=F32) / den


def _swa_prompt_kernel(sink_ref, qk_ref, kprev_ref, v_ref, vprev_ref, o_ref):
    n = pl.program_id(1)
    w = WINDOW
    qk = qk_ref[...]
    kcat = jnp.concatenate([kprev_ref[...], qk[:, SWA_Q:SWA_Q + SWA_KV]], axis=0)
    vcat = jnp.concatenate([vprev_ref[...], v_ref[...]], axis=0)
    qi = lax.broadcasted_iota(jnp.int32, (w, 2 * w), 0)
    kj = lax.broadcasted_iota(jnp.int32, (w, 2 * w), 1)
    valid = (kj <= qi + w) & (kj > qi) & ((kj >= w) | (n > 0))
    def head_chain(hh, k16, v16):
        s = _mm_nt(qk[:, hh * HEAD_DIM:(hh + 1) * HEAD_DIM], k16) * SCALE
        yield
        sink = sink_ref[hh]
        s = jnp.where(valid, s, NEG_INF)
        m = jnp.maximum(jnp.max(s, axis=-1, keepdims=True), sink)
        yield
        p = jnp.exp(s - m)
        den = jnp.sum(p, axis=-1, keepdims=True) + jnp.exp(sink - m)
        yield
        o_ref[:, hh * HEAD_DIM:(hh + 1) * HEAD_DIM] = (
            jnp.dot(p.astype(BF16), v16, preferred_element_type=F32) / den)

    chains = []
    for h in range(SWA_KV_HEADS):
        k16 = kcat[:, h * HEAD_DIM:(h + 1) * HEAD_DIM].astype(BF16)
        v16 = vcat[:, h * HEAD_DIM:(h + 1) * HEAD_DIM].astype(BF16)
        chains += [head_chain(h * SWA_GROUP + g, k16, v16) for g in range(SWA_GROUP)]
    _lockstep(chains)


def _swa_prompt(qk, v, sinks, batch, seq):
    nq = seq // WINDOW
    kblk = SWA_Q // SWA_KV

    def cur(b, n):
        return (b * nq + n, 0)

    def prev(b, n):
        return (b * nq + jnp.maximum(n - 1, 0), 0)

    return pl.pallas_call(
        _swa_prompt_kernel,
        name="swa_prompt",
        grid=(batch, nq),
        in_specs=[pl.BlockSpec(memory_space=pltpu.SMEM),
                  pl.BlockSpec((WINDOW, SWA_Q + SWA_KV), cur),
                  pl.BlockSpec((WINDOW, SWA_KV), lambda b, n: (b * nq + jnp.maximum(n - 1, 0), kblk)),
                  pl.BlockSpec((WINDOW, SWA_KV), cur),
                  pl.BlockSpec((WINDOW, SWA_KV), prev)],
        out_specs=pl.BlockSpec((WINDOW, SWA_Q), cur),
        out_shape=jax.ShapeDtypeStruct((batch * seq, SWA_Q), F32),
        compiler_params=_params("parallel", "parallel"),
    )(sinks, qk, qk, v, v)


def _swa_sample_kernel(t_new, sink_ref, qk_ref, v_ref, ck_ref, cv_ref, o_ref, nk_ref, nv_ref):
    per_tile = SUBLANES // t_new
    n_tiles = qk_ref.shape[0] // SUBLANES
    w = WINDOW
    nkeys = w + SUBLANES
    r8 = lax.broadcasted_iota(jnp.int32, (SUBLANES, SWA_KV), 0)
    rows = SWA_GROUP * SUBLANES
    rr = lax.broadcasted_iota(jnp.int32, (rows, nkeys), 0)
    jj = lax.broadcasted_iota(jnp.int32, (rows, nkeys), 1)
    r = rr % SUBLANES
    t = r % t_new
    rcol = lax.broadcasted_iota(jnp.int32, (rows, 1), 0)
    r8o = lax.broadcasted_iota(jnp.int32, (SUBLANES, HEAD_DIM), 0)
    for p in range(n_tiles):
        qk8 = qk_ref[p * SUBLANES:(p + 1) * SUBLANES, :]
        v8 = v_ref[p * SUBLANES:(p + 1) * SUBLANES, :]
        knew8 = qk8[:, SWA_Q:SWA_Q + SWA_KV]
        o_seq = []
        for s in range(per_tile):
            i = p * per_tile + s
            kb = ck_ref[i]
            vb = cv_ref[i]
            shift = (SUBLANES - t_new - s * t_new) % SUBLANES
            src_k = knew8 if shift == 0 else pltpu.roll(knew8, shift, 0)
            src_v = v8 if shift == 0 else pltpu.roll(v8, shift, 0)
            rk = pltpu.roll(kb, w - t_new, 0)
            rv = pltpu.roll(vb, w - t_new, 0)
            nk_ref[i, 0:w - SUBLANES, :] = rk[0:w - SUBLANES]
            nv_ref[i, 0:w - SUBLANES, :] = rv[0:w - SUBLANES]
            nk_ref[i, w - SUBLANES:w, :] = jnp.where(r8 >= SUBLANES - t_new, src_k, rk[w - SUBLANES:w])
            nv_ref[i, w - SUBLANES:w, :] = jnp.where(r8 >= SUBLANES - t_new, src_v, rv[w - SUBLANES:w])
            kext = jnp.concatenate([kb, knew8], axis=0)
            vext = jnp.concatenate([vb, v8], axis=0)
            jn = jj - w
            valid = ((jj < w) & (jj > t)) | ((jj >= w) & ((jn // t_new) == s) & ((jn % t_new) <= t))
            outs = []
            for h in range(SWA_KV_HEADS):
                k16 = kext[:, h * HEAD_DIM:(h + 1) * HEAD_DIM].astype(BF16)
                v16 = vext[:, h * HEAD_DIM:(h + 1) * HEAD_DIM].astype(BF16)
                qh = jnp.concatenate(
                    [qk8[:, (h * SWA_GROUP + g) * HEAD_DIM:(h * SWA_GROUP + g + 1) * HEAD_DIM]
                     for g in range(SWA_GROUP)], axis=0)
                sinkv = jnp.zeros((rows, 1), F32)
                for g in range(SWA_GROUP):
                    sinkv = jnp.where(rcol // SUBLANES == g, sink_ref[h * SWA_GROUP + g], sinkv)
                s_ = _mm_nt(qh, k16) * SCALE
                outs.append(_sink_attention(s_, valid, sinkv, v16))
            o_seq.append(outs)
        for h in range(SWA_KV_HEADS):
            for g in range(SWA_GROUP):
                tile = o_seq[0][h][g * SUBLANES:(g + 1) * SUBLANES]
                for s in range(1, per_tile):
                    tile = jnp.where(r8o // t_new == s, o_seq[s][h][g * SUBLANES:(g + 1) * SUBLANES], tile)
                hh = h * SWA_GROUP + g
                o_ref[p * SUBLANES:(p + 1) * SUBLANES, hh * HEAD_DIM:(hh + 1) * HEAD_DIM] = tile


def _swa_sample(qk, v, ck, cv, sinks, layer_e, nseq, t_new):
    assert SUBLANES % t_new == 0
    bb = SUBLANES
    while nseq % bb:
        bb //= 2
    assert (bb * t_new) % SUBLANES == 0
    rows = bb * t_new
    cache_spec = pl.BlockSpec((None, bb, WINDOW, SWA_KV), lambda i: (layer_e, i, 0, 0))
    out_cache = pl.BlockSpec((bb, WINDOW, SWA_KV), lambda i: (i, 0, 0))
    return pl.pallas_call(
        functools.partial(_swa_sample_kernel, t_new),
        name="swa_sample",
        grid=(nseq // bb,),
        in_specs=[pl.BlockSpec(memory_space=pltpu.SMEM),
                  pl.BlockSpec((rows, SWA_Q + SWA_KV), lambda i: (i, 0)),
                  pl.BlockSpec((rows, SWA_KV), lambda i: (i, 0)),
                  cache_spec, cache_spec],
        out_specs=[pl.BlockSpec((rows, SWA_Q), lambda i: (i, 0)), out_cache, out_cache],
        out_shape=[jax.ShapeDtypeStruct((nseq * t_new, SWA_Q), F32),
                   jax.ShapeDtypeStruct((nseq, WINDOW, SWA_KV), F32),
                   jax.ShapeDtypeStruct((nseq, WINDOW, SWA_KV), F32)],
        compiler_params=_params("parallel"),
    )(sinks, qk, v, ck, cv)


def _gdn_gates(ab, alog_row, dtb_row):
    g_all = -jnp.exp(alog_row) * _softplus(ab + dtb_row)
    beta_all = jax.nn.sigmoid(ab)
    return g_all, beta_all


def _cumsum_rows(x):
    n = x.shape[0]
    r = lax.broadcasted_iota(jnp.int32, x.shape, 0)
    s = 1
    while s < n:
        x = x + jnp.where(r >= s, pltpu.roll(x, s, 0), 0.0)
        s *= 2
    return x


GDN_INV_BASE = 8
GDN_SEQS_PER_STEP = 4
GDN_SAMPLE_UNROLL = 2


def _lockstep(chains):
    chains = list(chains)
    while chains:
        alive = []
        for chain in chains:
            try:
                next(chain)
                alive.append(chain)
            except StopIteration:
                pass
        chains = alive


def _unit_lower_inverse(lmat, ri, cj, eye, n):
    bsz = GDN_INV_BASE
    same = (ri // bsz) == (cj // bsz)
    pw = jnp.where(same, -lmat, 0.0)
    inv = eye + pw
    for _ in range((bsz - 1).bit_length() - 1):
        pw = _mm(pw, pw)
        yield
        inv = inv + _mm(inv, pw)
        yield
    while bsz < n:
        merged = (ri // (2 * bsz)) == (cj // (2 * bsz))
        off = jnp.where(merged & jnp.logical_not(same), lmat, 0.0)
        right = _mm(off, inv)
        yield
        inv = inv - _mm(inv, right)
        yield
        same = merged
        bsz *= 2
    return inv


def _gdn_prompt_kernel(x_ref, gz_ref, ab_ref, cw_ref, alog_ref, dtb_ref, gn_ref, o_ref, s_out_ref,
                       xs_scr, s_scr):
    _lockstep(_gdn_prompt_chunk(x_ref.at[sq], gz_ref.at[sq], ab_ref.at[sq], cw_ref, alog_ref, dtb_ref, gn_ref,
                                o_ref.at[sq], s_out_ref.at[sq], xs_scr.at[sq], s_scr.at[sq])
              for sq in range(x_ref.shape[0]))


def _gdn_prompt_chunk(x_ref, gz_ref, ab_ref, cw_ref, alog_ref, dtb_ref, gn_ref, o_ref, s_out_ref,
                      xs_scr, s_scr):
    c = pl.program_id(1)
    ch = GDN_CHUNK

    @pl.when(c == 0)
    def _():
        xs_scr[0:SUBLANES, :] = jnp.zeros((SUBLANES, GDN_CONV_CH), F32)
        s_scr[...] = jnp.zeros_like(s_scr)

    xs_scr[SUBLANES:SUBLANES + ch, :] = x_ref[...]
    cw = cw_ref[...]
    base = SUBLANES - (CONV_WIDTH - 1)
    conv = xs_scr[base:base + ch, :] * cw[0:1, :]
    for j in range(1, CONV_WIDTH):
        conv = conv + xs_scr[base + j:base + j + ch, :] * cw[j:j + 1, :]
    xs_scr[0:SUBLANES, :] = xs_scr[ch:ch + SUBLANES, :]
    qkv = _silu(conv)
    yield

    g_all, beta_all = _gdn_gates(ab_ref[...], alog_ref[...], dtb_ref[...])
    gcum = _cumsum_rows(g_all)

    nh = GDN_HEADS
    rows = nh * ch
    heads = range(nh)
    q_s = jnp.concatenate([_l2norm(qkv[:, h * GDN_DK:(h + 1) * GDN_DK]) * (GDN_DK ** -0.5) for h in heads], axis=0)
    k_s = jnp.concatenate([_l2norm(qkv[:, GDN_QK + h * GDN_DK:GDN_QK + (h + 1) * GDN_DK]) for h in heads], axis=0)
    v_s = jnp.concatenate([qkv[:, 2 * GDN_QK + h * GDN_DV:2 * GDN_QK + (h + 1) * GDN_DV] for h in heads], axis=0)
    gc = jnp.concatenate([gcum[:, h:h + 1] for h in heads], axis=0)
    beta = jnp.concatenate([beta_all[:, nh + h:nh + h + 1] for h in heads], axis=0)
    glast = jnp.concatenate([jnp.broadcast_to(gcum[ch - 1:ch, h:h + 1], (ch, 1)) for h in heads], axis=0)
    gr = jnp.broadcast_to(gc, (rows, LANES)).T[0:1, :]
    yield

    ri = lax.broadcasted_iota(jnp.int32, (rows, rows), 0)
    cj = lax.broadcasted_iota(jnp.int32, (rows, rows), 1)
    same_head = (ri // ch) == (cj // ch)
    eye = (ri == cj).astype(F32)
    decay = jnp.where(same_head & (cj <= ri), jnp.exp(gc - gr), 0.0)
    eg = jnp.exp(gc)
    kbeta = k_s * beta
    lmat = jnp.where(cj < ri, _mm_nt(kbeta, k_s) * decay, 0.0)
    yield
    inv = yield from _unit_lower_inverse(lmat, ri, cj, eye, ch)
    wu = _mm(inv, jnp.concatenate([kbeta * eg, v_s * beta], axis=1))
    yield
    w_s = wu[:, :GDN_DK]
    u_s = wu[:, GDN_DK:]
    attn = _mm_nt(q_s, k_s) * decay
    yield

    hrow = lax.broadcasted_iota(jnp.int32, (rows, 1), 0) // ch
    hcol = lax.broadcasted_iota(jnp.int32, (1, rows), 1) // ch

    def by_head(a):
        return jnp.concatenate([jnp.where(hrow == j, a, 0.0) for j in heads], axis=1)

    s_prev = s_scr[...]
    ws_qs = _mm(jnp.concatenate([by_head(w_s), by_head(q_s * eg)], axis=0), s_prev)
    yield
    v_new = u_s - ws_qs[:rows]
    o_s = ws_qs[rows:] + _mm(attn, v_new)
    yield
    kdec_t = (k_s * jnp.exp(glast - gc)).T
    kdec_bd = jnp.concatenate([jnp.where(hcol == j, kdec_t, 0.0) for j in heads], axis=0)
    gl_rows = jnp.concatenate(
        [jnp.broadcast_to(jnp.exp(gcum[ch - 1:ch, h:h + 1]), (GDN_DK, 1)) for h in heads], axis=0)
    s_scr[...] = s_prev * gl_rows + _mm(kdec_bd, v_new)
    yield

    for h in heads:
        gz = gz_ref[:, h * GDN_DV:(h + 1) * GDN_DV]
        o_ref[:, h * GDN_DV:(h + 1) * GDN_DV] = _rms(o_s[h * ch:(h + 1) * ch], gn_ref[...]) * _silu(gz)

    @pl.when(c == pl.num_programs(1) - 1)
    def _():
        s_out_ref[...] = s_scr[...].reshape(nh, GDN_DK, GDN_DV)


def _gdn_prompt(conv_in, gz, ab, cw, alog, dtb, gn, layer_e, batch, seq):
    ch = GDN_CHUNK
    nch = seq // ch
    grp = GDN_SEQS_PER_STEP if batch % GDN_SEQS_PER_STEP == 0 else 1
    nb = batch // grp

    def rows(width):
        return pl.BlockSpec((None, grp, ch, width), lambda b, c: (b, 0, c, 0))

    def par(shape):
        return pl.BlockSpec((None,) + shape, lambda b, c: (layer_e, 0, 0))

    ob, s_new = pl.pallas_call(
        _gdn_prompt_kernel,
        name="gdn_prompt",
        grid=(nb, nch),
        in_specs=[rows(GDN_CONV_CH), rows(GDN_V), rows(LANES),
                  par((CONV_WIDTH, GDN_CONV_CH)), par((1, LANES)), par((1, LANES)), par((1, GDN_DV))],
        out_specs=[rows(GDN_V),
                   pl.BlockSpec((None, grp, GDN_HEADS, GDN_DK, GDN_DV), lambda b, c: (b, 0, 0, 0, 0))],
        out_shape=[jax.ShapeDtypeStruct((nb, grp, seq, GDN_V), F32),
                   jax.ShapeDtypeStruct((nb, grp, GDN_HEADS, GDN_DK, GDN_DV), F32)],
        scratch_shapes=[pltpu.VMEM((grp, SUBLANES + ch, GDN_CONV_CH), F32),
                        pltpu.VMEM((grp, GDN_HEADS * GDN_DK, GDN_DV), F32)],
        compiler_params=_params("parallel", "arbitrary"),
    )(conv_in.reshape(nb, grp, seq, GDN_CONV_CH), gz.reshape(nb, grp, seq, GDN_V),
      ab.reshape(nb, grp, seq, LANES), cw, alog, dtb, gn)
    return ob.reshape(batch * seq, GDN_V), s_new.reshape(batch, GDN_HEADS, GDN_DK, GDN_DV)


def _gdn_sample_kernel(t_new, xx_ref, gz_ref, ab_ref, s_ref, cw_ref, alog_ref, dtb_ref, gn_ref,
                       o_ref, so_ref):
    first = SUBLANES - t_new
    cw = cw_ref[...]
    gn = gn_ref[...]

    def head_chain(i, h, qkv, a_all, beta_all, gz):
        q = _l2norm(qkv[:, h * GDN_DK:(h + 1) * GDN_DK]) * (GDN_DK ** -0.5)
        k = _l2norm(qkv[:, GDN_QK + h * GDN_DK:GDN_QK + (h + 1) * GDN_DK])
        v = qkv[:, 2 * GDN_QK + h * GDN_DV:2 * GDN_QK + (h + 1) * GDN_DV]
        q_t = q.T
        k_t = k.T
        kcs = [jnp.broadcast_to(k_t[:, first + tok:first + tok + 1], (GDN_DK, GDN_DV)) for tok in range(t_new)]
        qcs = [jnp.broadcast_to(q_t[:, first + tok:first + tok + 1], (GDN_DK, GDN_DV)) for tok in range(t_new)]
        s = s_ref[i, h]
        yield
        for tok in range(t_new):
            r = first + tok
            kc = kcs[tok]
            qc = qcs[tok]
            a = a_all[r:r + 1, h:h + 1]
            b = beta_all[r:r + 1, GDN_HEADS + h:GDN_HEADS + h + 1]
            kts = jnp.sum(s * kc, axis=0, keepdims=True)
            s = a * s + kc * (b * (v[r:r + 1, :] - a * kts))
            o = jnp.sum(s * qc, axis=0, keepdims=True)
            o_ref[i, r:r + 1, h * GDN_DV:(h + 1) * GDN_DV] = (
                _rms(o, gn) * _silu(gz[r:r + 1, h * GDN_DV:(h + 1) * GDN_DV]))
            yield
        so_ref[i, h] = s

    def seq_chains(i):
        xx = xx_ref[i]
        conv = xx * cw[CONV_WIDTH - 1:CONV_WIDTH, :]
        for d in range(1, CONV_WIDTH):
            conv = conv + pltpu.roll(xx, d, 0) * cw[CONV_WIDTH - 1 - d:CONV_WIDTH - d, :]
        qkv = _silu(conv)
        g_all, beta_all = _gdn_gates(ab_ref[i], alog_ref[...], dtb_ref[...])
        a_all = jnp.exp(g_all)
        o_ref[i, 0:first, :] = jnp.zeros((first, GDN_V), F32)
        return [head_chain(i, h, qkv, a_all, beta_all, gz_ref[i]) for h in range(GDN_HEADS)]

    def group(ii, carry):
        chains = []
        for u in range(GDN_SAMPLE_UNROLL):
            chains += seq_chains(ii * GDN_SAMPLE_UNROLL + u)
        _lockstep(chains)
        return carry

    assert xx_ref.shape[0] % GDN_SAMPLE_UNROLL == 0
    lax.fori_loop(0, xx_ref.shape[0] // GDN_SAMPLE_UNROLL, group, 0)


def _gdn_sample(xx8, gz8, ab8, state, cw, alog, dtb, gn, layer_e, nseq, t_new):
    bb = 16
    while nseq % bb:
        bb //= 2

    def par(shape):
        return pl.BlockSpec((None,) + shape, lambda i: (layer_e, 0, 0))

    return pl.pallas_call(
        functools.partial(_gdn_sample_kernel, t_new),
        name="gdn_sample",
        grid=(nseq // bb,),
        in_specs=[pl.BlockSpec((bb, SUBLANES, GDN_CONV_CH), lambda i: (i, 0, 0)),
                  pl.BlockSpec((bb, SUBLANES, GDN_V), lambda i: (i, 0, 0)),
                  pl.BlockSpec((bb, SUBLANES, LANES), lambda i: (i, 0, 0)),
                  pl.BlockSpec((None, bb, GDN_HEADS, GDN_DK, GDN_DV), lambda i: (layer_e, i, 0, 0, 0)),
                  par((CONV_WIDTH, GDN_CONV_CH)), par((1, LANES)), par((1, LANES)), par((1, GDN_DV))],
        out_specs=[pl.BlockSpec((bb, SUBLANES, GDN_V), lambda i: (i, 0, 0)),
                   pl.BlockSpec((bb, GDN_HEADS, GDN_DK, GDN_DV), lambda i: (i, 0, 0, 0))],
        out_shape=[jax.ShapeDtypeStruct((nseq, SUBLANES, GDN_V), F32),
                   jax.ShapeDtypeStruct((nseq, GDN_HEADS, GDN_DK, GDN_DV), F32)],
        compiler_params=_params("parallel"),
    )(xx8, gz8, ab8, state, cw, alog, dtb, gn)


def _moba_prompt_kernel(nblk, topk, q_ref, k_ref, vt_ref, o_ref, kmean_scr, m_scr, l_scr, acc_scr):
    qt = pl.program_id(1)
    blk = MOBA_BLOCK
    tq = q_ref.shape[0]
    nq = MOBA_GROUP * tq
    nrow = kmean_scr.shape[0]

    @pl.when(qt == 0)
    def _():
        kmean_scr[...] = jnp.zeros_like(kmean_scr)
        for n in range(nblk):
            kmean_scr[n:n + 1, :] = jnp.mean(k_ref[n * blk:(n + 1) * blk, :], axis=0, keepdims=True)

    own = qt
    own_start = pl.multiple_of(own * blk, blk)
    nidx = lax.broadcasted_iota(jnp.int32, (nrow, nq), 0)
    kk = lax.broadcasted_iota(jnp.int32, (blk, tq), 0)
    qi = lax.broadcasted_iota(jnp.int32, (blk, tq), 1)

    heads = range(MOBA_KV_HEADS)
    lanes = [slice(h * HEAD_DIM, (h + 1) * HEAD_DIM) for h in heads]
    qhs, qh16s = [], []
    for h in heads:
        qh = jnp.concatenate(
            [q_ref[:, (h * MOBA_GROUP + g) * HEAD_DIM:(h * MOBA_GROUP + g + 1) * HEAD_DIM]
             for g in range(MOBA_GROUP)], axis=0)
        qh = qh * (SCALE * LOG2_E)
        qhs.append(qh)
        qh16s.append(qh.astype(BF16))

    def own_chain(h, g, k_own, vt_own):
        cols = slice(g * tq, (g + 1) * tq)
        s_t = _mm_nt(k_own, qh16s[h][g * tq:(g + 1) * tq])
        yield
        s_t = jnp.where(kk <= qi, s_t, NEG_INF)
        m0 = jnp.max(s_t, axis=0, keepdims=True)
        yield
        p = jnp.exp2(s_t - m0)
        m_scr[h, :, cols] = m0
        l_scr[h, :, cols] = jnp.sum(p, axis=0, keepdims=True)
        yield
        acc_scr[h, :, cols] = _mm(vt_own, p)

    chains = []
    for h in heads:
        k_own = k_ref[pl.ds(own_start, blk), lanes[h]].astype(BF16)
        vt_own = vt_ref[lanes[h], pl.ds(own_start, blk)].astype(BF16)
        chains += [own_chain(h, g, k_own, vt_own) for g in range(MOBA_GROUP)]
    _lockstep(chains)

    if topk > 0:
        sel_ts = []
        for h in heads:
            gate_t = _mm_nt(kmean_scr[:, lanes[h]], qhs[h])
            gm = jnp.where(nidx < own, gate_t, NEG_INF)
            sel_t = jnp.zeros((nrow, nq), F32)
            for n in range(nblk - 1):
                row = gm[n:n + 1, :]
                beats = jnp.where(gm > row, 1.0, jnp.where((gm == row) & (nidx < n), 1.0, 0.0))
                cnt = jnp.sum(beats, axis=0, keepdims=True)
                keep = jnp.where(cnt < topk, 1.0, 0.0) * jnp.where(n < own, 1.0, 0.0)
                sel_t = jnp.where(nidx == n, keep, sel_t)
            sel_ts.append(sel_t)

        def past(n, carry):
            start = pl.multiple_of(n * blk, blk)

            def past_chain(h, g, k_n, vt_n, sel_row):
                cols = slice(g * tq, (g + 1) * tq)
                s_n = _mm_nt(k_n, qh16s[h][g * tq:(g + 1) * tq])
                yield
                s_n = jnp.where(sel_row[:, cols] > 0.0, s_n, NEG_INF)
                m_old = m_scr[h, :, cols]
                m_new = jnp.maximum(m_old, jnp.max(s_n, axis=0, keepdims=True))
                yield
                alpha = jnp.exp2(m_old - m_new)
                p_n = jnp.exp2(s_n - m_new)
                l_scr[h, :, cols] = alpha * l_scr[h, :, cols] + jnp.sum(p_n, axis=0, keepdims=True)
                m_scr[h, :, cols] = m_new
                yield
                acc_scr[h, :, cols] = alpha * acc_scr[h, :, cols] + _mm(vt_n, p_n)

            chains = []
            for h in heads:
                sel_row = jnp.sum(jnp.where(nidx == n, sel_ts[h], 0.0), axis=0, keepdims=True)
                k_n = k_ref[pl.ds(start, blk), lanes[h]].astype(BF16)
                vt_n = vt_ref[lanes[h], pl.ds(start, blk)].astype(BF16)
                chains += [past_chain(h, g, k_n, vt_n, sel_row) for g in range(MOBA_GROUP)]
            _lockstep(chains)
            return carry

        lax.fori_loop(0, own, past, 0)

    for h in heads:
        o_t = acc_scr[h] / l_scr[h]
        for pair in range(MOBA_GROUP // 2):
            two = jnp.concatenate([o_t[:, (2 * pair) * tq:(2 * pair + 1) * tq],
                                   o_t[:, (2 * pair + 1) * tq:(2 * pair + 2) * tq]], axis=0)
            col = h * MOBA_GROUP * HEAD_DIM + pair * LANES
            o_ref[:, col:col + LANES] = two.T


def _moba_prompt(q, k, vt, batch, seq):
    blk = MOBA_BLOCK
    assert seq % blk == 0
    nblk = seq // blk
    topk = min(MOBA_TOPK, nblk - 1)
    nrow = -(-nblk // SUBLANES) * SUBLANES
    nq = MOBA_GROUP * blk
    return pl.pallas_call(
        functools.partial(_moba_prompt_kernel, nblk, topk),
        name="moba_prompt",
        grid=(batch, nblk),
        in_specs=[pl.BlockSpec((blk, MOBA_Q), lambda b, t: (b * nblk + t, 0)),
                  pl.BlockSpec((seq, MOBA_KV), lambda b, t: (b, 0)),
                  pl.BlockSpec((None, MOBA_KV, seq), lambda b, t: (b, 0, 0))],
        out_specs=pl.BlockSpec((blk, MOBA_Q), lambda b, t: (b * nblk + t, 0)),
        out_shape=jax.ShapeDtypeStruct((batch * seq, MOBA_Q), F32),
        scratch_shapes=[pltpu.VMEM((nrow, MOBA_KV), F32), pltpu.VMEM((MOBA_KV_HEADS, 1, nq), F32),
                        pltpu.VMEM((MOBA_KV_HEADS, 1, nq), F32), pltpu.VMEM((MOBA_KV_HEADS, HEAD_DIM, nq), F32)],
        compiler_params=_params("parallel", "arbitrary"),
    )(q, k, vt)


def _moba_sample_kernel(layer, n_pages, topk, t_new, pt_ref, q_ref, kn_ref, vn_ref, pk_ref, pv_ref,
                        o_ref, kbuf, vbuf, s_scr, sem):
    b = pl.program_id(0)
    nb = pl.num_programs(0)
    blk = MOBA_BLOCK
    ppb = blk // PAGE_SIZE
    n_full = n_pages // ppb
    nq = q_ref.shape[1]

    def page_copies(seq_idx, slot, p):
        pg = pt_ref[seq_idx * n_pages + p]
        return (pltpu.make_async_copy(pk_ref.at[layer, pg], kbuf.at[slot, p], sem.at[0, slot]),
                pltpu.make_async_copy(pv_ref.at[layer, pg], vbuf.at[slot, p], sem.at[1, slot]))

    def fetch(seq_idx, slot):
        for p in range(n_pages):
            ck, cv = page_copies(seq_idx, slot, p)
            ck.start()
            cv.start()

    slot = b % 2

    @pl.when(b == 0)
    def _():
        fetch(0, 0)

    @pl.when(b + 1 < nb)
    def _():
        fetch(b + 1, 1 - slot)

    for p in range(n_pages):
        ck, cv = page_copies(b, slot, p)
        ck.wait()
        cv.wait()

    q16 = q_ref[0].astype(BF16)
    lane = lax.broadcasted_iota(jnp.int32, (nq, LANES), 1)

    s_own = _mm_nt(q16, kn_ref[0]) * SCALE
    rr = lax.broadcasted_iota(jnp.int32, s_own.shape, 0)
    jj = lax.broadcasted_iota(jnp.int32, s_own.shape, 1)
    tq = (rr % (t_new * MOBA_GROUP)) // MOBA_GROUP
    s_own = jnp.where((jj <= tq) & (jj < t_new), s_own, NEG_INF)
    m = jnp.max(s_own, axis=1, keepdims=True)

    if topk > 0:
        for p in range(n_pages):
            s_scr[:, p * PAGE_SIZE:(p + 1) * PAGE_SIZE] = _mm(q16, kbuf[slot, p])
        gate = jnp.full((nq, LANES), NEG_INF, F32)
        bmax = gate
        for n in range(n_full):
            s = s_scr[:, n * blk:(n + 1) * blk]
            gate = jnp.where(lane == n, jnp.sum(s, axis=1, keepdims=True) * (1.0 / blk), gate)
            bmax = jnp.where(lane == n, jnp.max(s, axis=1, keepdims=True), bmax)
        sel = jnp.zeros((nq, LANES), F32)
        for _ in range(topk):
            mx = jnp.max(gate, axis=1, keepdims=True)
            idx = jnp.min(jnp.where(gate == mx, lane, LANES), axis=1, keepdims=True)
            hit = lane == idx
            sel = jnp.where(hit, 1.0, sel)
            gate = jnp.where(hit, NEG_INF, gate)
        m = jnp.maximum(m, jnp.max(jnp.where(sel > 0.0, bmax * SCALE, NEG_INF), axis=1, keepdims=True))

    p_own = jnp.exp(s_own - m)
    l0 = jnp.sum(p_own, axis=1, keepdims=True)
    vn = vn_ref[0]
    acc0 = p_own[:, 0:1] * vn[0:1, :]
    for j in range(1, t_new):
        acc0 = acc0 + p_own[:, j:j + 1] * vn[j:j + 1, :]

    if topk > 0:
        psum = jnp.zeros((nq, PAGE_SIZE), F32)
        for n in range(n_full):
            chosen = sel[:, n:n + 1] > 0.0
            for pg in range(n * ppb, (n + 1) * ppb):
                s = s_scr[:, pg * PAGE_SIZE:(pg + 1) * PAGE_SIZE] * SCALE
                p = jnp.where(chosen, jnp.exp(s - m), 0.0)
                psum = psum + p
                acc0 = acc0 + _mm_nt(p, vbuf[slot, pg])
        l0 = l0 + jnp.sum(psum, axis=1, keepdims=True)

    o_ref[0] = acc0 / l0


def _moba_sample(q_bd, kn8, vn8, pool_k, pool_v, page_table, layer, t_new):
    nseq, nq, _ = q_bd.shape
    n_pages = page_table.shape[1]
    ppb = MOBA_BLOCK // PAGE_SIZE
    assert n_pages % ppb == 0
    n_full = n_pages // ppb
    assert n_full <= LANES
    topk = min(MOBA_TOPK, n_full)
    grid_spec = pltpu.PrefetchScalarGridSpec(
        num_scalar_prefetch=1,
        grid=(nseq,),
        in_specs=[pl.BlockSpec((1, nq, MOBA_KV), lambda b, pt: (b, 0, 0)),
                  pl.BlockSpec((1, SUBLANES, MOBA_KV), lambda b, pt: (b, 0, 0)),
                  pl.BlockSpec((1, SUBLANES, MOBA_KV), lambda b, pt: (b, 0, 0)),
                  pl.BlockSpec(memory_space=pl.ANY),
                  pl.BlockSpec(memory_space=pl.ANY)],
        out_specs=pl.BlockSpec((1, nq, MOBA_KV), lambda b, pt: (b, 0, 0)),
        scratch_shapes=[pltpu.VMEM((2, n_pages, MOBA_KV, PAGE_SIZE), F32),
                        pltpu.VMEM((2, n_pages, MOBA_KV, PAGE_SIZE), F32),
                        pltpu.VMEM((nq, max(n_full, 1) * MOBA_BLOCK), F32),
                        pltpu.SemaphoreType.DMA((2, 2))])
    return pl.pallas_call(
        functools.partial(_moba_sample_kernel, layer, n_pages, topk, t_new),
        name="moba_sample",
        grid_spec=grid_spec,
        out_shape=jax.ShapeDtypeStruct((nseq, nq, MOBA_KV), F32),
        compiler_params=_params("arbitrary"),
    )(page_table.reshape(-1), q_bd, kn8, vn8, pool_k, pool_v)


def _rope_tables(pos):
    half = HEAD_DIM // 2
    inv = ROPE_THETA ** (-jnp.arange(half, dtype=F32) / half)
    ang = pos.astype(F32)[:, None] * inv[None, :]
    cos, sin = jnp.cos(ang), jnp.sin(ang)
    reps = LANES // HEAD_DIM
    cos_t = jnp.tile(jnp.concatenate([cos, cos], axis=1), (1, reps))
    sin_t = jnp.tile(jnp.concatenate([-sin, sin], axis=1), (1, reps))
    return cos_t, sin_t


EVEN_SEGS = ((0, SWA_Q + SWA_KV, True), (SWA_Q + SWA_KV, SWA_KV, False),
             (SWA_Q + 2 * SWA_KV, GDN_CONV_CH, False),
             (SWA_Q + 2 * SWA_KV + GDN_CONV_CH, GDN_V, False), (EVEN_MAIN, LANES, False))
ODD_SEGS = ((0, MOBA_Q, True), (MOBA_Q, MOBA_KV, True), (MOBA_Q + MOBA_KV, MOBA_KV, False))


def _pages_t(pool):
    n_layers, n_phys = pool.shape[:2]
    return jnp.transpose(pool, (0, 1, 3, 4, 2)).reshape(n_layers, n_phys, MOBA_KV, PAGE_SIZE)


def _pad_rows8(a, t_new):
    nseq = a.shape[0] // t_new
    a = a.reshape(nseq, t_new, a.shape[1])
    return jnp.pad(a, ((0, 0), (SUBLANES - t_new, 0), (0, 0)))


def _trunk(grp, x, batch, seq, wts, caches):
    (norm_g4, w1, w3, w2, w_even, w_out_even, sinks, conv_w, alog_p, dtb_p, gnorm, w_odd, w_out_odd,
     final_g) = wts
    depth = w1.shape[0]
    swa_k, swa_v, conv_out, gdn_out, moba_k, moba_v = [], [], [], [], [], []
    for l in range(depth):
        x = _ffn(grp, x, l, 0, 0, norm_g4, w1, w3, w2)
        if l % 2 == 0:
            e = l // 2
            qk, va, conv_in, gz, ab = _proj(grp, x, l, 3, norm_g4, w_even, e, EVEN_SEGS)
            if grp.prompt:
                oa = _swa_prompt(qk, va, sinks[e], batch, seq)
                ka = qk[:, SWA_Q:].reshape(batch, seq, SWA_KV_HEADS, HEAD_DIM)
                swa_k.append(ka[:, seq - WINDOW:])
                swa_v.append(va.reshape(batch, seq, SWA_KV_HEADS, HEAD_DIM)[:, seq - WINDOW:])
                ob, s_new = _gdn_prompt(conv_in, gz, ab, conv_w, alog_p, dtb_p, gnorm, e, batch, seq)
                conv_out.append(conv_in.reshape(batch, seq, GDN_CONV_CH)[:, seq - (CONV_WIDTH - 1):])
            else:
                ck, cv, conv_st, gdn_st = caches[:4]
                oa, nk, nv = _swa_sample(qk, va, ck, cv, sinks[e], e, batch, seq)
                swa_k.append(nk.reshape(batch, WINDOW, SWA_KV_HEADS, HEAD_DIM))
                swa_v.append(nv.reshape(batch, WINDOW, SWA_KV_HEADS, HEAD_DIM))
                hist = jnp.concatenate([conv_st[e], conv_in.reshape(batch, seq, GDN_CONV_CH)], axis=1)
                conv_out.append(hist[:, seq:])
                xx8 = jnp.pad(hist, ((0, 0), (SUBLANES - hist.shape[1], 0), (0, 0)))
                ob8, s_new = _gdn_sample(xx8, _pad_rows8(gz, seq), _pad_rows8(ab, seq), gdn_st, conv_w,
                                         alog_p, dtb_p, gnorm, e, batch, seq)
                ob = ob8[:, SUBLANES - seq:].reshape(batch * seq, GDN_V)
            gdn_out.append(s_new)
            x = _outproj(grp, x, l, 5, [oa, ob], w_out_even, e)
        else:
            j = l // 2
            q, k, v = _proj(grp, x, l, 3, norm_g4, w_odd, j, ODD_SEGS)
            moba_k.append(k.reshape(batch, seq, MOBA_KV_HEADS, HEAD_DIM))
            moba_v.append(v.reshape(batch, seq, MOBA_KV_HEADS, HEAD_DIM))
            if grp.prompt:
                vt = jnp.swapaxes(v.reshape(batch, seq, MOBA_KV), 1, 2)
                o = _moba_prompt(q, k, vt, batch, seq)
            else:
                pool_k, pool_v, page_table = caches[4:]
                q5 = q.reshape(batch, seq, MOBA_KV_HEADS, MOBA_GROUP, HEAD_DIM).transpose(0, 2, 1, 3, 4)
                eye = jnp.eye(MOBA_KV_HEADS, dtype=F32)
                q_bd = (q5[:, :, :, :, None, :] * eye[None, :, None, None, :, None]).reshape(
                    batch, MOBA_KV_HEADS * seq * MOBA_GROUP, MOBA_KV)
                kn8 = jnp.pad(k.reshape(batch, seq, MOBA_KV), ((0, 0), (0, SUBLANES - seq), (0, 0)))
                vn8 = jnp.pad(v.reshape(batch, seq, MOBA_KV), ((0, 0), (0, SUBLANES - seq), (0, 0)))
                o_bd = _moba_sample(q_bd, kn8, vn8, pool_k, pool_v, page_table, j, seq)
                o6 = o_bd.reshape(batch, MOBA_KV_HEADS, seq, MOBA_GROUP, MOBA_KV_HEADS, HEAD_DIM)
                idx = jnp.arange(MOBA_KV_HEADS)
                o = o6[:, idx, :, :, idx]
                o = o.transpose(1, 2, 0, 3, 4).reshape(batch * seq, MOBA_Q)
            x = _outproj(grp, x, l, 5, [o], w_out_odd, j)
        x = _ffn(grp, x, l, 1, 6, norm_g4, w1, w3, w2, final_g if l == depth - 1 else None)
    return (x, jnp.stack(swa_k), jnp.stack(swa_v), jnp.stack(conv_out), jnp.stack(gdn_out),
            jnp.stack(moba_k), jnp.stack(moba_v))


def kernel(x_prompt, x_sample, cache_swa_k, cache_swa_v, state_conv, state_gdn, cache_moba_k, cache_moba_v,
           page_table, c_prompt, c_sample, w_ada, b_ada, norm_g, ffn_w1, ffn_w3, ffn_w2, w_in_even,
           w_out_even, swa_sinks, conv_w, a_log, dt_bias, gdn_norm_g, w_in_odd, w_out_odd, final_norm_g):
    batch, seq, d = x_prompt.shape
    nseq, t_new, _ = x_sample.shape
    depth = w_ada.shape[0]
    past_len = page_table.shape[1] * PAGE_SIZE
    assert past_len % MOBA_BLOCK == 0 and seq % MOBA_BLOCK == 0 and seq % GDN_CHUNK == 0
    assert t_new <= SUBLANES and t_new >= CONV_WIDTH - 1

    bp = -(-batch // SUBLANES) * SUBLANES
    c_p = jnp.pad(c_prompt, ((0, bp - batch), (0, 0)))
    c_s = jnp.repeat(c_sample, t_new, axis=0)
    mod_p, mod_s = _adaln(c_p, c_s, w_ada, b_ada)
    mod_p = mod_p.reshape(depth, bp, 1, N_MOD * d)

    n_even = w_in_even.shape[0]
    w_even = jnp.pad(w_in_even.astype(BF16), ((0, 0), (0, 0), (0, EVEN_MAIN + LANES - w_in_even.shape[2])))
    gate_pad = ((0, 0), (0, 0), (0, LANES - GDN_HEADS))
    wts = (norm_g.reshape(depth, 3, 1, d), ffn_w1.astype(BF16), ffn_w3.astype(BF16), ffn_w2.astype(BF16),
           w_even, w_out_even.astype(BF16), swa_sinks, conv_w,
           jnp.pad(a_log.reshape(n_even, 1, GDN_HEADS), gate_pad),
           jnp.pad(dt_bias.reshape(n_even, 1, GDN_HEADS), gate_pad),
           gdn_norm_g.reshape(n_even, 1, GDN_DV), w_in_odd.astype(BF16), w_out_odd.astype(BF16),
           final_norm_g.reshape(1, d))

    cos_p, sin_p = _rope_tables(jnp.arange(seq, dtype=jnp.int32))
    grp_p = _Group(True, batch * seq, seq, mod_p, cos_p, sin_p)
    outs_p = _trunk(grp_p, x_prompt.reshape(batch * seq, d), batch, seq, wts, None)

    cos_s, sin_s = _rope_tables(past_len + jnp.arange(t_new, dtype=jnp.int32))
    grp_s = _Group(False, nseq * t_new, t_new, mod_s, jnp.tile(cos_s, (nseq, 1)), jnp.tile(sin_s, (nseq, 1)))
    n_phys = cache_moba_k.shape[1]
    caches = (cache_swa_k.reshape(n_even, nseq, WINDOW, SWA_KV), cache_swa_v.reshape(n_even, nseq, WINDOW, SWA_KV),
              state_conv, state_gdn,
              _pages_t(cache_moba_k), _pages_t(cache_moba_v), page_table)
    outs_s = _trunk(grp_s, x_sample.reshape(nseq * t_new, d), nseq, t_new, wts, caches)

    y_p = outs_p[0].reshape(batch, seq, d)
    y_s = outs_s[0].reshape(nseq, t_new, d)
    return (y_p, y_s) + tuple(outs_p[1:]) + tuple(outs_s[1:])
```
